```python
import jax, jax.numpy as jnp
from jax import lax
import numpy as np

D_MODEL = 1024
BATCH = 8
SEQ = 4096
DEPTH = 2

GRID_W = 64
CTX_LEN = 256
CHUNK = 64
CONV_K = 5
GDN_HEADS = 4
GDN_DK = 128
GDN_DV = 128
RET_HEADS = 4
RET_DK = 128
RET_DV = 128
SSD_HEADS = 16
SSD_HEADDIM = 64
SSD_GROUPS = 2
SSD_STATE = 128
SSD_DINNER = SSD_HEADS * SSD_HEADDIM
D_FF = 2816
N_BRANCH = 3
N_ADA = 9
ROPE_BASE = 10000.0
GDN_QKV = GDN_HEADS * (2 * GDN_DK + GDN_DV)
SSD_XBC = SSD_DINNER + 2 * SSD_GROUPS * SSD_STATE
SPLIT_SIZES = (GDN_QKV, GDN_HEADS * GDN_DV, 2 * GDN_HEADS, 2 * GDN_HEADS,
               RET_HEADS * RET_DK, RET_HEADS * RET_DK, RET_HEADS * RET_DV, RET_HEADS * RET_DV,
               SSD_DINNER, SSD_XBC, 2 * SSD_HEADS, N_BRANCH * D_MODEL)
N_IN = sum(SPLIT_SIZES)

kernel_name = 'hybrid_gdn_retention_ssd_prefix_dit_block'


def layer_norm(x, g, b, eps=1e-5):
    xf = x.astype(jnp.float32)
    mu = jnp.mean(xf, axis=-1, keepdims=True)
    var = jnp.mean(jnp.square(xf - mu), axis=-1, keepdims=True)
    return ((xf - mu) * lax.rsqrt(var + eps) * g.astype(jnp.float32) + b.astype(jnp.float32)).astype(x.dtype)


def head_norm(x, eps=1e-5):
    xf = x.astype(jnp.float32)
    mu = jnp.mean(xf, axis=-1, keepdims=True)
    var = jnp.mean(jnp.square(xf - mu), axis=-1, keepdims=True)
    return (xf - mu) * lax.rsqrt(var + eps)


def rms_norm(x, g, eps=1e-6):
    xf = x.astype(jnp.float32)
    return xf * lax.rsqrt(jnp.mean(jnp.square(xf), axis=-1, keepdims=True) + eps) * g.astype(jnp.float32)


def l2norm(x, eps=1e-6):
    xf = x.astype(jnp.float32)
    return (xf * lax.rsqrt(jnp.sum(jnp.square(xf), axis=-1, keepdims=True) + eps)).astype(x.dtype)


def modulate(x, shift, scale):
    return x * (1 + scale) + shift


def swiglu(u, w13, w2):
    a, b = jnp.split(u @ w13, 2, axis=-1)
    return (jax.nn.silu(a) * b) @ w2


def flip(t):
    return jnp.flip(t, axis=1)


def dwconv_centred(x, w, b=None):
    pad = w.shape[0] // 2
    y = lax.conv_general_dilated(x, w[:, None, :], window_strides=(1,), padding=((pad, pad),),
                                 dimension_numbers=('NWC', 'WIO', 'NWC'),
                                 feature_group_count=x.shape[-1])
    return y if b is None else y + b


def axial_rope(seq_len):
    rows = seq_len // GRID_W
    row_id = jnp.repeat(jnp.arange(rows, dtype=jnp.float32), GRID_W)
    col_id = jnp.tile(jnp.arange(GRID_W, dtype=jnp.float32), rows)
    n_freq = RET_DK // 4
    inv_freq = ROPE_BASE ** (-jnp.arange(n_freq, dtype=jnp.float32) / n_freq)
    ang = jnp.concatenate([row_id[:, None] * inv_freq, col_id[:, None] * inv_freq], axis=-1)
    return jnp.cos(ang), jnp.sin(ang)


def apply_rope(x, cos, sin):
    half = x.shape[-1] // 2
    xf = x.astype(jnp.float32)
    x1, x2 = xf[..., :half], xf[..., half:]
    c, s = cos[None, :, None, :], sin[None, :, None, :]
    return jnp.concatenate([x1 * c - x2 * s, x2 * c + x1 * s], axis=-1).astype(x.dtype)


def chunk_delta_rule(q, k, v, beta, log_a, s0, with_output):
    b_, seq, nh, _ = q.shape
    dv = v.shape[-1]
    n = seq // CHUNK
    f32 = jnp.float32

    def to_chunks(t):
        return jnp.moveaxis(t.astype(f32).reshape(b_, n, CHUNK, nh, t.shape[-1]), (1, 3), (0, 2))

    qc, kc, vc = to_chunks(q), to_chunks(k), to_chunks(v)
    bc = to_chunks(beta[..., None])[..., 0]
    gc = jnp.cumsum(to_chunks(log_a[..., None])[..., 0], axis=-1)
    pos = jnp.arange(CHUNK)
    strict = pos[:, None] > pos[None, :]
    diff = gc[..., :, None] - gc[..., None, :]
    a_mat = (jnp.einsum('nbhik,nbhjk->nbhij', kc, kc)
             * jnp.where(strict, jnp.exp(jnp.where(strict, diff, 0.0)), 0.0) * bc[..., :, None])
    rhs = jnp.concatenate([bc[..., None] * vc, (bc * jnp.exp(gc))[..., None] * kc], axis=-1)
    sol = lax.linalg.triangular_solve(jnp.eye(CHUNK, dtype=f32) + a_mat, rhs, left_side=True, lower=True)
    w_v, w_k = sol[..., :dv], sol[..., dv:]
    k_w = kc * jnp.exp(gc[..., -1:] - gc)[..., None]
    c_dec = jnp.exp(gc[..., -1])
    if with_output:
        incl = pos[:, None] >= pos[None, :]
        p_mat = (jnp.einsum('nbhik,nbhjk->nbhij', qc, kc)
                 * jnp.where(incl, jnp.exp(jnp.where(incl, diff, 0.0)), 0.0))
        q_w = qc * jnp.exp(gc)[..., None]
        xs = (w_v, w_k, k_w, c_dec, q_w, p_mat)
    else:
        xs = (w_v, w_k, k_w, c_dec)

    def step(s, xc):
        u_c = xc[0] - jnp.einsum('bhik,bhkv->bhiv', xc[1], s)
        s_new = xc[3][..., None, None] * s + jnp.einsum('bhjk,bhjv->bhkv', xc[2], u_c)
        if with_output:
            o_c = jnp.einsum('bhik,bhkv->bhiv', xc[4], s) + jnp.einsum('bhij,bhjv->bhiv', xc[5], u_c)
            return s_new, o_c
        return s_new, None

    s_fin, o = lax.scan(step, s0.astype(f32), xs)
    if not with_output:
        return None, s_fin
    return jnp.moveaxis(o, (0, 2), (1, 3)).reshape(b_, seq, nh, dv), s_fin


def chunk_gla(q, k, v, log_a, s0, with_output):
    b_, seq, ng, kd = q.shape
    hg, dv = v.shape[3], v.shape[4]
    n = seq // CHUNK
    f32 = jnp.float32
    qc = jnp.moveaxis(q.astype(f32).reshape(b_, n, CHUNK, ng, kd), (1, 3), (0, 2))
    kc = jnp.moveaxis(k.astype(f32).reshape(b_, n, CHUNK, ng, kd), (1, 3), (0, 2))
    vc = jnp.moveaxis(v.astype(f32).reshape(b_, n, CHUNK, ng, hg, dv), (1, 3, 4), (0, 2, 3))
    gc = jnp.cumsum(jnp.moveaxis(log_a.astype(f32).reshape(b_, n, CHUNK, ng, hg), (1, 3, 4), (0, 2, 3)),
                    axis=-1)
    v_w = vc * jnp.exp(gc[..., -1:] - gc)[..., None]
    c_dec = jnp.exp(gc[..., -1])
    if with_output:
        pos = jnp.arange(CHUNK)
        incl = pos[:, None] >= pos[None, :]
        diff = gc[..., :, None] - gc[..., None, :]
        dmat = jnp.where(incl, jnp.exp(jnp.where(incl, diff, 0.0)), 0.0)
        scores = jnp.einsum('nbgik,nbgjk->nbgij', qc, kc)
        intra = jnp.einsum('nbghij,nbghjv->nbghiv', scores[:, :, :, None] * dmat, vc)
        q_dec = jnp.exp(gc)
        xs = (kc, v_w, c_dec, qc, q_dec)
    else:
        xs = (kc, v_w, c_dec)

    def step(s, xc):
        s_new = xc[2][..., None, None] * s + jnp.einsum('bgjk,bghjv->bghkv', xc[0], xc[1])
        if with_output:
            o_c = jnp.einsum('bgik,bghkv->bghiv', xc[3], s) * xc[4][..., None]
            return s_new, o_c
        return s_new, None

    s_fin, inter = lax.scan(step, s0.astype(f32), xs)
    if not with_output:
        return None, s_fin
    o = jnp.moveaxis(intra + inter, (0, 4), (1, 2)).reshape(b_, seq, ng, hg, dv)
    return o, s_fin


def zero_states(b_):
    f32 = jnp.float32
    hg = SSD_HEADS // SSD_GROUPS
    z_gdn = jnp.zeros((b_, GDN_HEADS, GDN_DK, GDN_DV), f32)
    z_ret = jnp.zeros((b_, RET_HEADS, 1, RET_DK, RET_DV), f32)
    z_ssd = jnp.zeros((b_, SSD_GROUPS, hg, SSD_STATE, SSD_HEADDIM), f32)
    return (z_gdn, z_gdn, z_ret, z_ret, z_ssd, z_ssd)


def token_mix(u, states, rope, lp, with_output):
    b_, seq, d = u.shape
    f32 = jnp.float32
    idx = [int(s) for s in np.cumsum(SPLIT_SIZES)[:-1]]
    (g_qkv, g_z, g_b, g_a, r_q, r_k, r_v, r_g, s_z, s_xbc, s_dt, br_gate) = jnp.split(
        u @ lp['w_in'], idx, axis=-1)

    qkv = jax.nn.silu(dwconv_centred(g_qkv, lp['gdn_conv_w']))
    gq, gk, gv = jnp.split(qkv, [GDN_HEADS * GDN_DK, 2 * GDN_HEADS * GDN_DK], axis=-1)
    gq = l2norm(gq.reshape(b_, seq, GDN_HEADS, GDN_DK)) * GDN_DK ** -0.5
    gk = l2norm(gk.reshape(b_, seq, GDN_HEADS, GDN_DK))
    gv = gv.reshape(b_, seq, GDN_HEADS, GDN_DV)
    g_beta = jax.nn.sigmoid(g_b.astype(f32)).reshape(b_, seq, 2, GDN_HEADS)
    g_loga = -jnp.exp(lp['gdn_a_log'].astype(f32)) * jax.nn.softplus(
        g_a.astype(f32).reshape(b_, seq, 2, GDN_HEADS) + lp['gdn_dt_bias'].astype(f32))
    oa_f, sa_f = chunk_delta_rule(gq, gk, gv, g_beta[:, :, 0], g_loga[:, :, 0], states[0], with_output)
    oa_b, sa_b = chunk_delta_rule(flip(gq), flip(gk), flip(gv), flip(g_beta[:, :, 1]), flip(g_loga[:, :, 1]),
                                  states[1], with_output)

    rq = r_q.reshape(b_, seq, RET_HEADS, RET_DK)
    rk = r_k.reshape(b_, seq, RET_HEADS, RET_DK) * RET_DK ** -0.5
    if rope is not None:
        rq = apply_rope(rq, rope[0], rope[1])
        rk = apply_rope(rk, rope[0], rope[1])
    rv = r_v.reshape(b_, seq, RET_HEADS, 1, RET_DV)
    r_loga = -jnp.exp(lp['ret_decay'].astype(f32))
    la_f = jnp.broadcast_to(r_loga[0][:, None], (b_, seq, RET_HEADS, 1))
    la_b = jnp.broadcast_to(r_loga[1][:, None], (b_, seq, RET_HEADS, 1))
    ob_f, sb_f = chunk_gla(rq, rk, rv, la_f, states[2], with_output)
    ob_b, sb_b = chunk_gla(flip(rq), flip(rk), flip(rv), la_b, states[3], with_output)

    hg = SSD_HEADS // SSD_GROUPS
    xbc = jax.nn.silu(dwconv_centred(s_xbc, lp['ssd_conv_w'], lp['ssd_conv_b']))
    sx, sb, sc = jnp.split(xbc, [SSD_DINNER, SSD_DINNER + SSD_GROUPS * SSD_STATE], axis=-1)
    sx = sx.reshape(b_, seq, SSD_GROUPS, hg, SSD_HEADDIM)
    sb = sb.reshape(b_, seq, SSD_GROUPS, SSD_STATE)
    sc = sc.reshape(b_, seq, SSD_GROUPS, SSD_STATE)
    delta = jax.nn.softplus(s_dt.astype(f32).reshape(b_, seq, 2, SSD_HEADS) + lp['ssd_dt_bias'].astype(f32))
    s_loga = (delta * -jnp.exp(lp['ssd_a_log'].astype(f32))).reshape(b_, seq, 2, SSD_GROUPS, hg)
    delta = delta.reshape(b_, seq, 2, SSD_GROUPS, hg)
    oc_f, sc_f = chunk_gla(sc, sb, sx * delta[:, :, 0, :, :, None], s_loga[:, :, 0], states[4], with_output)
    oc_b, sc_b = chunk_gla(flip(sc), flip(sb), flip(sx * delta[:, :, 1, :, :, None]), flip(s_loga[:, :, 1]),
                           states[5], with_output)

    new_states = (sa_f, sa_b, sb_f, sb_b, sc_f, sc_b)
    if not with_output:
        return None, new_states

    oa = oa_f + flip(oa_b)
    ya = rms_norm(oa, lp['gdn_norm_g']) * jax.nn.silu(g_z.astype(f32).reshape(b_, seq, GDN_HEADS, GDN_DV))
    ob = (ob_f + flip(ob_b))[:, :, :, 0]
    yb = (head_norm(ob) * lp['ret_norm_g'].astype(f32).reshape(RET_HEADS, RET_DV)
          * jax.nn.silu(r_g.astype(f32).reshape(b_, seq, RET_HEADS, RET_DV)))
    oc = oc_f + flip(oc_b) + lp['ssd_d'].astype(f32).reshape(SSD_GROUPS, hg)[:, :, None] * sx
    oc = oc.reshape(b_, seq, SSD_DINNER) * jax.nn.silu(s_z.astype(f32))
    yc = rms_norm(oc.reshape(b_, seq, SSD_GROUPS, SSD_DINNER // SSD_GROUPS),
                  lp['ssd_norm_g'].reshape(SSD_GROUPS, SSD_DINNER // SSD_GROUPS))
    ya = ya.reshape(b_, seq, -1).astype(u.dtype)
    yb = yb.reshape(b_, seq, -1).astype(u.dtype)
    yc = yc.reshape(b_, seq, -1).astype(u.dtype)
    gates = jax.nn.sigmoid(br_gate).reshape(b_, seq, N_BRANCH, d)
    merged = (gates[:, :, 0] * (ya @ lp['w_br_a']) + gates[:, :, 1] * (yb @ lp['w_br_b'])
              + gates[:, :, 2] * (yc @ lp['w_br_c']))
    return merged @ lp['w_out'], new_states


def setup_inputs(seed: int = 0) -> dict:
    key = jax.random.key(seed)
    ks = jax.random.split(key, 32)
    f32 = jnp.float32
    beta_dn = (8 * DEPTH) ** -0.25

    def nrm(k, shape, scale):
        return jax.random.normal(k, shape, f32) * scale

    def dt_bias(k, shape):
        dt = jnp.exp(jax.random.uniform(k, shape, f32, np.log(1e-3), np.log(1e-1)))
        return dt + jnp.log(-jnp.expm1(-dt))

    ret_base = jnp.asarray(np.log(-np.log(1.0 - 2.0 ** (-5.0 - np.arange(RET_HEADS)))), f32)
    return {
        'x': nrm(ks[0], (BATCH, SEQ, D_MODEL), 1.0),
        'c': nrm(ks[1], (BATCH, D_MODEL), 1.0),
        'ctx': nrm(ks[2], (BATCH, CTX_LEN, D_MODEL), 1.0),
        'c_ctx': nrm(ks[3], (D_MODEL,), 1.0),
        'ada_w': nrm(ks[4], (DEPTH, D_MODEL, N_ADA * D_MODEL), D_MODEL ** -0.5),
        'ada_b': nrm(ks[5], (DEPTH, N_ADA * D_MODEL), 0.02),
        'ln_g': 1.0 + nrm(ks[6], (DEPTH, 3, D_MODEL), 0.02),
        'ln_b': nrm(ks[7], (DEPTH, 3, D_MODEL), 0.02),
        'ffn_w13': nrm(ks[8], (DEPTH, 2, D_MODEL, 2 * D_FF), D_MODEL ** -0.5),
        'ffn_w2': nrm(ks[9], (DEPTH, 2, D_FF, D_MODEL), D_FF ** -0.5 * beta_dn),
        'mix_w_in': nrm(ks[10], (DEPTH, D_MODEL, N_IN), D_MODEL ** -0.5),
        'gdn_conv_w': nrm(ks[11], (DEPTH, CONV_K, GDN_QKV), CONV_K ** -0.5),
        'gdn_a_log': jnp.log(jax.random.uniform(ks[12], (DEPTH, 2, GDN_HEADS), f32, 1.0, 16.0)),
        'gdn_dt_bias': dt_bias(ks[13], (DEPTH, 2, GDN_HEADS)),
        'gdn_norm_g': 1.0 + nrm(ks[14], (DEPTH, GDN_DV), 0.02),
        'ret_decay': ret_base + nrm(ks[15], (DEPTH, 2, RET_HEADS), 0.05),
        'ret_norm_g': 1.0 + nrm(ks[16], (DEPTH, RET_HEADS * RET_DV), 0.02),
        'ssd_conv_w': nrm(ks[17], (DEPTH, CONV_K, SSD_XBC), CONV_K ** -0.5),
        'ssd_conv_b': nrm(ks[18], (DEPTH, SSD_XBC), 0.02),
        'ssd_a_log': jnp.log(jax.random.uniform(ks[19], (DEPTH, 2, SSD_HEADS), f32, 1.0, 16.0)),
        'ssd_dt_bias': dt_bias(ks[20], (DEPTH, 2, SSD_HEADS)),
        'ssd_d': 1.0 + nrm(ks[21], (DEPTH, SSD_HEADS), 0.02),
        'ssd_norm_g': 1.0 + nrm(ks[22], (DEPTH, SSD_DINNER), 0.02),
        'w_br_a': nrm(ks[23], (DEPTH, GDN_HEADS * GDN_DV, D_MODEL), (GDN_HEADS * GDN_DV) ** -0.5),
        'w_br_b': nrm(ks[24], (DEPTH, RET_HEADS * RET_DV, D_MODEL), (RET_HEADS * RET_DV) ** -0.5),
        'w_br_c': nrm(ks[25], (DEPTH, SSD_DINNER, D_MODEL), SSD_DINNER ** -0.5),
        'mix_w_out': nrm(ks[26], (DEPTH, D_MODEL, D_MODEL), D_MODEL ** -0.5 * beta_dn),
    }


def reference(x, c, ctx, c_ctx, ada_w, ada_b, ln_g, ln_b, ffn_w13, ffn_w2, mix_w_in,
              gdn_conv_w, gdn_a_log, gdn_dt_bias, gdn_norm_g, ret_decay, ret_norm_g,
              ssd_conv_w, ssd_conv_b, ssd_a_log, ssd_dt_bias, ssd_d, ssd_norm_g,
              w_br_a, w_br_b, w_br_c, mix_w_out):
    alpha = (2 * DEPTH) ** 0.25
    b_, seq, d = x.shape
    rope = axial_rope(seq)
    silu_c = jax.nn.silu(c)
    silu_cc = jax.nn.silu(c_ctx)
    h, hc = x, ctx
    for i in range(DEPTH):
        last = i == DEPTH - 1
        mod = (silu_c @ ada_w[i] + ada_b[i]).reshape(b_, 1, N_ADA, d)
        mod_c = (silu_cc @ ada_w[i] + ada_b[i]).reshape(N_ADA, d)
        lp = {'w_in': mix_w_in[i], 'gdn_conv_w': gdn_conv_w[i], 'gdn_a_log': gdn_a_log[i],
              'gdn_dt_bias': gdn_dt_bias[i], 'gdn_norm_g': gdn_norm_g[i], 'ret_decay': ret_decay[i],
              'ret_norm_g': ret_norm_g[i], 'ssd_conv_w': ssd_conv_w[i], 'ssd_conv_b': ssd_conv_b[i],
              'ssd_a_log': ssd_a_log[i], 'ssd_dt_bias': ssd_dt_bias[i], 'ssd_d': ssd_d[i],
              'ssd_norm_g': ssd_norm_g[i], 'w_br_a': w_br_a[i], 'w_br_b': w_br_b[i],
              'w_br_c': w_br_c[i], 'w_out': mix_w_out[i]}
        h = layer_norm(alpha * h + 0.5 * mod[:, :, 2] * swiglu(modulate(h, mod[:, :, 0], mod[:, :, 1]),
                                                             ffn_w13[i, 0], ffn_w2[i, 0]),
                       ln_g[i, 0], ln_b[i, 0])
        hc = layer_norm(alpha * hc + 0.5 * mod_c[2] * swiglu(modulate(hc, mod_c[0], mod_c[1]),
                                                            ffn_w13[i, 0], ffn_w2[i, 0]),
                        ln_g[i, 0], ln_b[i, 0])
        ctx_out, ctx_states = token_mix(modulate(hc, mod_c[3], mod_c[4]), zero_states(b_), None, lp, not last)
        lat_out, _ = token_mix(modulate(h, mod[:, :, 3], mod[:, :, 4]), ctx_states, rope, lp, True)
        h = layer_norm(alpha * h + mod[:, :, 5] * lat_out, ln_g[i, 1], ln_b[i, 1])
        h = layer_norm(alpha * h + 0.5 * mod[:, :, 8] * swiglu(modulate(h, mod[:, :, 6], mod[:, :, 7]),
                                                             ffn_w13[i, 1], ffn_w2[i, 1]),
                       ln_g[i, 2], ln_b[i, 2])
        if not last:
            hc = layer_norm(alpha * hc + mod_c[5] * ctx_out, ln_g[i, 1], ln_b[i, 1])
            hc = layer_norm(alpha * hc + 0.5 * mod_c[8] * swiglu(modulate(hc, mod_c[6], mod_c[7]),
                                                                ffn_w13[i, 1], ffn_w2[i, 1]),
                            ln_g[i, 2], ln_b[i, 2])
    return h
```

```python
import functools

import numpy as np
import jax
import jax.numpy as jnp
from jax import lax
from jax.experimental import pallas as pl
from jax.experimental.pallas import tpu as pltpu

F32 = jnp.float32
BF16 = jnp.bfloat16

GRID_W = 64
CHUNK = 64
CONV_K = 5
GDN_HEADS = 4
GDN_DK = 128
GDN_DV = 128
RET_HEADS = 4
RET_DK = 128
RET_DV = 128
SSD_HEADS = 16
SSD_HEADDIM = 64
SSD_GROUPS = 2
SSD_STATE = 128
SSD_DINNER = SSD_HEADS * SSD_HEADDIM
N_BRANCH = 3
N_ADA = 9
ROPE_BASE = 10000.0
GDN_QKV = GDN_HEADS * (2 * GDN_DK + GDN_DV)
SSD_XBC = SSD_DINNER + 2 * SSD_GROUPS * SSD_STATE
LN_EPS = 1e-5
RMS_EPS = 1e-6

VMEM_LIMIT_BYTES = 56 * 1024 * 1024
LANE = 128
SMALL_W = LANE


def _cparams(n_axes):
    return pltpu.CompilerParams(dimension_semantics=("arbitrary",) * n_axes,
                                vmem_limit_bytes=VMEM_LIMIT_BYTES)


def _resident(shape):
    nd = len(shape)
    return pl.BlockSpec(shape, lambda *_: (0,) * nd, pipeline_mode=pl.Buffered(1))


def _layer_norm_rows(y, g, b):
    mu = jnp.mean(y, axis=-1, keepdims=True)
    yc = y - mu
    var = jnp.mean(yc * yc, axis=-1, keepdims=True)
    return yc * lax.rsqrt(var + LN_EPS) * g + b


def _silu(x):
    return x * jax.nn.sigmoid(x)


def _ada_body(c_ref, w_ref, b_ref, o_ref):
    s = _silu(c_ref[...]).astype(BF16)
    o_ref[...] = jnp.dot(s, w_ref[...].astype(BF16), preferred_element_type=F32) + b_ref[...]


def ada_modulation(c_rows, w, b):
    r, d = c_rows.shape
    n = w.shape[1]
    tn = 1024
    return pl.pallas_call(
        _ada_body,
        grid=(n // tn,),
        in_specs=[pl.BlockSpec((r, d), lambda j: (0, 0)),
                  pl.BlockSpec((d, tn), lambda j: (0, j)),
                  pl.BlockSpec((1, tn), lambda j: (0, j))],
        out_specs=pl.BlockSpec((r, tn), lambda j: (0, j)),
        out_shape=jax.ShapeDtypeStruct((r, n), F32),
        compiler_params=_cparams(1),
        name="ada_modulation",
    )(c_rows, w, b)


def _ffn_body(h_ref, mod_ref, w13_ref, w2_ref, lng_ref, lnb_ref, o_ref, *, alpha, ff, chunks, mod_base):
    h = h_ref[...]
    shift = mod_ref[mod_base:mod_base + 1, :]
    scale = mod_ref[mod_base + 1:mod_base + 2, :]
    gate = mod_ref[mod_base + 2:mod_base + 3, :]
    u = (h * (1.0 + scale) + shift).astype(BF16)
    acc = None
    for c0, c1 in chunks:
        a = jnp.dot(u, w13_ref[:, c0:c1], preferred_element_type=F32)
        b = jnp.dot(u, w13_ref[:, ff + c0:ff + c1], preferred_element_type=F32)
        g = (_silu(a) * b).astype(BF16)
        p = jnp.dot(g, w2_ref[c0:c1, :], preferred_element_type=F32)
        acc = p if acc is None else acc + p
    y = alpha * h + (0.5 * gate) * acc
    o_ref[...] = _layer_norm_rows(y, lng_ref[...], lnb_ref[...])


def _ff_chunks(ff, width=1024):
    return tuple((c0, min(c0 + width, ff)) for c0 in range(0, ff, width))


def ffn_sublayer(h, mod, w13, w2, ln_g, ln_b, *, alpha, mod_base, tm):
    b_, seq, d = h.shape
    ff = w2.shape[0]
    tm = min(tm, seq)
    body = functools.partial(_ffn_body, alpha=alpha, ff=ff, chunks=_ff_chunks(ff), mod_base=mod_base)
    return pl.pallas_call(
        body,
        grid=(b_, seq // tm),
        in_specs=[pl.BlockSpec((None, tm, d), lambda b, i: (b, i, 0)),
                  pl.BlockSpec((None, N_ADA, d), lambda b, i: (b, 0, 0)),
                  _resident(w13.shape), _resident(w2.shape),
                  _resident((1, d)), _resident((1, d))],
        out_specs=pl.BlockSpec((None, tm, d), lambda b, i: (b, i, 0)),
        out_shape=jax.ShapeDtypeStruct(h.shape, F32),
        compiler_params=_cparams(2),
        name="ffn_sublayer",
    )(h, mod, w13, w2, ln_g.reshape(1, d), ln_b.reshape(1, d))


PROJ_GROUPS = (("g_qkv", GDN_QKV, BF16), ("g_z", GDN_HEADS * GDN_DV, BF16),
               ("r_q", RET_HEADS * RET_DK, BF16), ("r_k", RET_HEADS * RET_DK, BF16),
               ("r_v", RET_HEADS * RET_DV, BF16), ("r_g", RET_HEADS * RET_DV, BF16),
               ("s_z", SSD_DINNER, BF16), ("s_xbc", SSD_XBC, BF16),
               ("br_gate", None, BF16), ("small", SMALL_W, F32))


def _proj_groups(d):
    return tuple((n, (N_BRANCH * d if w is None else w), t) for n, w, t in PROJ_GROUPS)


def reorder_w_in(w_in, d):
    sizes = (GDN_QKV, GDN_HEADS * GDN_DV, 2 * GDN_HEADS, 2 * GDN_HEADS,
             RET_HEADS * RET_DK, RET_HEADS * RET_DK, RET_HEADS * RET_DV, RET_HEADS * RET_DV,
             SSD_DINNER, SSD_XBC, 2 * SSD_HEADS, N_BRANCH * d)
    offs = np.concatenate([[0], np.cumsum(sizes)])
    cols = [w_in[:, offs[k]:offs[k + 1]] for k in range(len(sizes))]
    (g_qkv, g_z, g_b, g_a, r_q, r_k, r_v, r_g, s_z, s_xbc, s_dt, br_gate) = cols
    n_small = 4 * GDN_HEADS + 2 * SSD_HEADS
    small = jnp.concatenate([g_b, g_a, s_dt, jnp.zeros((w_in.shape[0], SMALL_W - n_small), w_in.dtype)], axis=1)
    return jnp.concatenate([g_qkv, g_z, r_q, r_k, r_v, r_g, s_z, s_xbc, br_gate, small], axis=1).astype(BF16)


def _inproj_body(h_ref, mod_ref, w_ref, *o_refs, groups):
    shift = mod_ref[3:4, :]
    scale = mod_ref[4:5, :]
    u = (h_ref[...] * (1.0 + scale) + shift).astype(BF16)
    c0 = 0
    for o_ref, (_, width, dtype) in zip(o_refs, groups):
        o_ref[...] = jnp.dot(u, w_ref[:, c0:c0 + width], preferred_element_type=F32).astype(dtype)
        c0 += width


def mixer_in_projection(h, mod, w_in_r, *, tm):
    b_, seq, d = h.shape
    groups = _proj_groups(d)
    tm = min(tm, seq)
    body = functools.partial(_inproj_body, groups=groups)
    outs = pl.pallas_call(
        body,
        grid=(b_, seq // tm),
        in_specs=[pl.BlockSpec((None, tm, d), lambda b, i: (b, i, 0)),
                  pl.BlockSpec((None, N_ADA, d), lambda b, i: (b, 0, 0)),
                  _resident(w_in_r.shape)],
        out_specs=[pl.BlockSpec((None, tm, w), lambda b, i: (b, i, 0)) for _, w, _ in groups],
        out_shape=[jax.ShapeDtypeStruct((b_, seq, w), t) for _, w, t in groups],
        compiler_params=_cparams(2),
        name="mixer_in_projection",
    )(h, mod, w_in_r)
    return {n: o for (n, _, _), o in zip(groups, outs)}


def _merge_body(h_ref, mod_ref, oaf_ref, oab_ref, obf_ref, obb_ref, ocf_ref, ocb_ref, sx_ref,
                gz_ref, rg_ref, sz_ref, brg_ref, gng_ref, rng_ref, sd_ref, sng_ref,
                wa_ref, wb_ref, wc_ref, wo_ref, lng_ref, lnb_ref, o_ref, *, alpha):
    d = h_ref.shape[-1]
    oa = oaf_ref[...] + oab_ref[...]
    gz = gz_ref[...].astype(F32)
    ya = []
    for hh in range(GDN_HEADS):
        x = oa[:, hh * GDN_DV:(hh + 1) * GDN_DV]
        ms = jnp.mean(x * x, axis=-1, keepdims=True)
        ya.append(x * lax.rsqrt(ms + RMS_EPS) * gng_ref[...] * _silu(gz[:, hh * GDN_DV:(hh + 1) * GDN_DV]))
    ya = jnp.concatenate(ya, axis=-1).astype(BF16)
    ob = obf_ref[...] + obb_ref[...]
    rg = rg_ref[...].astype(F32)
    yb = []
    for hh in range(RET_HEADS):
        sl = slice(hh * RET_DV, (hh + 1) * RET_DV)
        x = ob[:, sl]
        mu = jnp.mean(x, axis=-1, keepdims=True)
        xc = x - mu
        var = jnp.mean(xc * xc, axis=-1, keepdims=True)
        yb.append(xc * lax.rsqrt(var + LN_EPS) * rng_ref[:, sl] * _silu(rg[:, sl]))
    yb = jnp.concatenate(yb, axis=-1).astype(BF16)
    oc = (ocf_ref[...] + ocb_ref[...] + sd_ref[...] * sx_ref[...].astype(F32)) * _silu(sz_ref[...].astype(F32))
    gw = SSD_DINNER // SSD_GROUPS
    yc = []
    for gg in range(SSD_GROUPS):
        sl = slice(gg * gw, (gg + 1) * gw)
        x = oc[:, sl]
        ms = jnp.mean(x * x, axis=-1, keepdims=True)
        yc.append(x * lax.rsqrt(ms + RMS_EPS) * sng_ref[:, sl])
    yc = jnp.concatenate(yc, axis=-1).astype(BF16)
    gates = jax.nn.sigmoid(brg_ref[...].astype(F32))
    merged = (gates[:, 0:d] * jnp.dot(ya, wa_ref[...], preferred_element_type=F32)
              + gates[:, d:2 * d] * jnp.dot(yb, wb_ref[...], preferred_element_type=F32)
              + gates[:, 2 * d:3 * d] * jnp.dot(yc, wc_ref[...], preferred_element_type=F32))
    mix = jnp.dot(merged.astype(BF16), wo_ref[...], preferred_element_type=F32)
    h = h_ref[...]
    y = alpha * h + mod_ref[5:6, :] * mix
    o_ref[...] = _layer_norm_rows(y, lng_ref[...], lnb_ref[...])


def mixer_merge(h, mod, o, proj, sx, lp, ln_g, ln_b, *, alpha, tm):
    b_, seq, d = h.shape
    tm = min(tm, seq)

    def tok(w):
        return pl.BlockSpec((None, tm, w), lambda b, i: (b, i, 0))

    wa, wb, wc, wo = lp["w_br_a"], lp["w_br_b"], lp["w_br_c"], lp["w_out"]
    hg = SSD_HEADS // SSD_GROUPS
    ssd_d_cols = jnp.repeat(lp["ssd_d"].astype(F32), SSD_HEADDIM).reshape(1, SSD_DINNER)
    args = [h, mod, o["a_f"], o["a_b"], o["b_f"], o["b_b"], o["c_f"], o["c_b"], sx,
            proj["g_z"], proj["r_g"], proj["s_z"], proj["br_gate"],
            lp["gdn_norm_g"].astype(F32).reshape(1, GDN_DV),
            lp["ret_norm_g"].astype(F32).reshape(1, RET_HEADS * RET_DV),
            ssd_d_cols, lp["ssd_norm_g"].astype(F32).reshape(1, SSD_DINNER),
            wa, wb, wc, wo, ln_g.reshape(1, d), ln_b.reshape(1, d)]
    del hg
    in_specs = [tok(d), pl.BlockSpec((None, N_ADA, d), lambda b, i: (b, 0, 0)),
                tok(GDN_HEADS * GDN_DV), tok(GDN_HEADS * GDN_DV),
                tok(RET_HEADS * RET_DV), tok(RET_HEADS * RET_DV),
                tok(SSD_DINNER), tok(SSD_DINNER), tok(SSD_DINNER),
                tok(GDN_HEADS * GDN_DV), tok(RET_HEADS * RET_DV), tok(SSD_DINNER), tok(N_BRANCH * d)]
    in_specs += [_resident(a.shape) for a in args[13:]]
    return pl.pallas_call(
        functools.partial(_merge_body, alpha=alpha),
        grid=(b_, seq // tm),
        in_specs=in_specs,
        out_specs=tok(d),
        out_shape=jax.ShapeDtypeStruct(h.shape, F32),
        compiler_params=_cparams(2),
        name="mixer_merge",
    )(*args)


def _flip(t):
    return jnp.flip(t, axis=1)


def _dwconv_centred(x, w, b=None):
    pad = w.shape[0] // 2
    y = lax.conv_general_dilated(x, w[:, None, :], window_strides=(1,), padding=((pad, pad),),
                                 dimension_numbers=('NWC', 'WIO', 'NWC'),
                                 feature_group_count=x.shape[-1])
    return y if b is None else y + b


def _l2norm(x, eps=1e-6):
    return x * lax.rsqrt(jnp.sum(jnp.square(x), axis=-1, keepdims=True) + eps)


def _axial_rope(seq_len):
    rows = seq_len // GRID_W
    row_id = jnp.repeat(jnp.arange(rows, dtype=F32), GRID_W)
    col_id = jnp.tile(jnp.arange(GRID_W, dtype=F32), rows)
    n_freq = RET_DK // 4
    inv_freq = ROPE_BASE ** (-jnp.arange(n_freq, dtype=F32) / n_freq)
    ang = jnp.concatenate([row_id[:, None] * inv_freq, col_id[:, None] * inv_freq], axis=-1)
    return jnp.cos(ang), jnp.sin(ang)


def _apply_rope(x, cos, sin):
    half = x.shape[-1] // 2
    x1, x2 = x[..., :half], x[..., half:]
    c, s = cos[None, :, None, :], sin[None, :, None, :]
    return jnp.concatenate([x1 * c - x2 * s, x2 * c + x1 * s], axis=-1)


def _chunk_delta_rule(q, k, v, beta, log_a, s0, with_output):
    b_, seq, nh, _ = q.shape
    dv = v.shape[-1]
    n = seq // CHUNK

    def to_chunks(t):
        return jnp.moveaxis(t.astype(F32).reshape(b_, n, CHUNK, nh, t.shape[-1]), (1, 3), (0, 2))

    qc, kc, vc = to_chunks(q), to_chunks(k), to_chunks(v)
    bc = to_chunks(beta[..., None])[..., 0]
    gc = jnp.cumsum(to_chunks(log_a[..., None])[..., 0], axis=-1)
    pos = jnp.arange(CHUNK)
    strict = pos[:, None] > pos[None, :]
    diff = gc[..., :, None] - gc[..., None, :]
    a_mat = (jnp.einsum('nbhik,nbhjk->nbhij', kc, kc)
             * jnp.where(strict, jnp.exp(jnp.where(strict, diff, 0.0)), 0.0) * bc[..., :, None])
    rhs = jnp.concatenate([bc[..., None] * vc, (bc * jnp.exp(gc))[..., None] * kc], axis=-1)
    sol = lax.linalg.triangular_solve(jnp.eye(CHUNK, dtype=F32) + a_mat, rhs, left_side=True, lower=True)
    w_v, w_k = sol[..., :dv], sol[..., dv:]
    k_w = kc * jnp.exp(gc[..., -1:] - gc)[..., None]
    c_dec = jnp.exp(gc[..., -1])
    if with_output:
        incl = pos[:, None] >= pos[None, :]
        p_mat = (jnp.einsum('nbhik,nbhjk->nbhij', qc, kc)
                 * jnp.where(incl, jnp.exp(jnp.where(incl, diff, 0.0)), 0.0))
        q_w = qc * jnp.exp(gc)[..., None]
        xs = (w_v, w_k, k_w, c_dec, q_w, p_mat)
    else:
        xs = (w_v, w_k, k_w, c_dec)

    def step(s, xc):
        u_c = xc[0] - jnp.einsum('bhik,bhkv->bhiv', xc[1], s)
        s_new = xc[3][..., None, None] * s + jnp.einsum('bhjk,bhjv->bhkv', xc[2], u_c)
        if with_output:
            o_c = jnp.einsum('bhik,bhkv->bhiv', xc[4], s) + jnp.einsum('bhij,bhjv->bhiv', xc[5], u_c)
            return s_new, o_c
        return s_new, None

    s_fin, o = lax.scan(step, s0.astype(F32), xs)
    if not with_output:
        return None, s_fin
    return jnp.moveaxis(o, (0, 2), (1, 3)).reshape(b_, seq, nh, dv), s_fin


def _chunk_gla(q, k, v, log_a, s0, with_output):
    b_, seq, ng, kd = q.shape
    hg, dv = v.shape[3], v.shape[4]
    n = seq // CHUNK
    qc = jnp.moveaxis(q.astype(F32).reshape(b_, n, CHUNK, ng, kd), (1, 3), (0, 2))
    kc = jnp.moveaxis(k.astype(F32).reshape(b_, n, CHUNK, ng, kd), (1, 3), (0, 2))
    vc = jnp.moveaxis(v.astype(F32).reshape(b_, n, CHUNK, ng, hg, dv), (1, 3, 4), (0, 2, 3))
    gc = jnp.cumsum(jnp.moveaxis(log_a.astype(F32).reshape(b_, n, CHUNK, ng, hg), (1, 3, 4), (0, 2, 3)), axis=-1)
    v_w = vc * jnp.exp(gc[..., -1:] - gc)[..., None]
    c_dec = jnp.exp(gc[..., -1])
    if with_output:
        pos = jnp.arange(CHUNK)
        incl = pos[:, None] >= pos[None, :]
        diff = gc[..., :, None] - gc[..., None, :]
        dmat = jnp.where(incl, jnp.exp(jnp.where(incl, diff, 0.0)), 0.0)
        scores = jnp.einsum('nbgik,nbgjk->nbgij', qc, kc)
        intra = jnp.einsum('nbghij,nbghjv->nbghiv', scores[:, :, :, None] * dmat, vc)
        q_dec = jnp.exp(gc)
        xs = (kc, v_w, c_dec, qc, q_dec)
    else:
        xs = (kc, v_w, c_dec)

    def step(s, xc):
        s_new = xc[2][..., None, None] * s + jnp.einsum('bgjk,bghjv->bghkv', xc[0], xc[1])
        if with_output:
            o_c = jnp.einsum('bgik,bghkv->bghiv', xc[3], s) * xc[4][..., None]
            return s_new, o_c
        return s_new, None

    s_fin, inter = lax.scan(step, s0.astype(F32), xs)
    if not with_output:
        return None, s_fin
    o = jnp.moveaxis(intra + inter, (0, 4), (1, 2)).reshape(b_, seq, ng, hg, dv)
    return o, s_fin


def _zero_states(b_):
    hg = SSD_HEADS // SSD_GROUPS
    z_gdn = jnp.zeros((b_, GDN_HEADS, GDN_DK, GDN_DV), F32)
    z_ret = jnp.zeros((b_, RET_HEADS, 1, RET_DK, RET_DV), F32)
    z_ssd = jnp.zeros((b_, SSD_GROUPS, hg, SSD_STATE, SSD_HEADDIM), F32)
    return (z_gdn, z_gdn, z_ret, z_ret, z_ssd, z_ssd)


def token_mixers(proj, states, rope, lp, with_output):
    b_, seq, _ = proj["g_qkv"].shape
    small = proj["small"]
    g_b = small[..., 0:2 * GDN_HEADS]
    g_a = small[..., 2 * GDN_HEADS:4 * GDN_HEADS]
    s_dt = small[..., 4 * GDN_HEADS:4 * GDN_HEADS + 2 * SSD_HEADS]

    qkv = jax.nn.silu(_dwconv_centred(proj["g_qkv"].astype(F32), lp['gdn_conv_w']))
    gq, gk, gv = jnp.split(qkv, [GDN_HEADS * GDN_DK, 2 * GDN_HEADS * GDN_DK], axis=-1)
    gq = _l2norm(gq.reshape(b_, seq, GDN_HEADS, GDN_DK)) * GDN_DK ** -0.5
    gk = _l2norm(gk.reshape(b_, seq, GDN_HEADS, GDN_DK))
    gv = gv.reshape(b_, seq, GDN_HEADS, GDN_DV)
    g_beta = jax.nn.sigmoid(g_b).reshape(b_, seq, 2, GDN_HEADS)
    g_loga = -jnp.exp(lp['gdn_a_log'].astype(F32)) * jax.nn.softplus(
        g_a.reshape(b_, seq, 2, GDN_HEADS) + lp['gdn_dt_bias'].astype(F32))
    oa_f, sa_f = _chunk_delta_rule(gq, gk, gv, g_beta[:, :, 0], g_loga[:, :, 0], states[0], with_output)
    oa_b, sa_b = _chunk_delta_rule(_flip(gq), _flip(gk), _flip(gv), _flip(g_beta[:, :, 1]),
                                   _flip(g_loga[:, :, 1]), states[1], with_output)

    rq = proj["r_q"].astype(F32).reshape(b_, seq, RET_HEADS, RET_DK)
    rk = proj["r_k"].astype(F32).reshape(b_, seq, RET_HEADS, RET_DK) * RET_DK ** -0.5
    if rope is not None:
        rq = _apply_rope(rq, rope[0], rope[1])
        rk = _apply_rope(rk, rope[0], rope[1])
    rv = proj["r_v"].astype(F32).reshape(b_, seq, RET_HEADS, 1, RET_DV)
    r_loga = -jnp.exp(lp['ret_decay'].astype(F32))
    la_f = jnp.broadcast_to(r_loga[0][:, None], (b_, seq, RET_HEADS, 1))
    la_b = jnp.broadcast_to(r_loga[1][:, None], (b_, seq, RET_HEADS, 1))
    ob_f, sb_f = _chunk_gla(rq, rk, rv, la_f, states[2], with_output)
    ob_b, sb_b = _chunk_gla(_flip(rq), _flip(rk), _flip(rv), la_b, states[3], with_output)

    hg = SSD_HEADS // SSD_GROUPS
    xbc = jax.nn.silu(_dwconv_centred(proj["s_xbc"].astype(F32), lp['ssd_conv_w'], lp['ssd_conv_b']))
    sx, sb, sc = jnp.split(xbc, [SSD_DINNER, SSD_DINNER + SSD_GROUPS * SSD_STATE], axis=-1)
    sx5 = sx.reshape(b_, seq, SSD_GROUPS, hg, SSD_HEADDIM)
    sb = sb.reshape(b_, seq, SSD_GROUPS, SSD_STATE)
    sc = sc.reshape(b_, seq, SSD_GROUPS, SSD_STATE)
    delta = jax.nn.softplus(s_dt.reshape(b_, seq, 2, SSD_HEADS) + lp['ssd_dt_bias'].astype(F32))
    s_loga = (delta * -jnp.exp(lp['ssd_a_log'].astype(F32))).reshape(b_, seq, 2, SSD_GROUPS, hg)
    delta = delta.reshape(b_, seq, 2, SSD_GROUPS, hg)
    oc_f, sc_f = _chunk_gla(sc, sb, sx5 * delta[:, :, 0, :, :, None], s_loga[:, :, 0], states[4], with_output)
    oc_b, sc_b = _chunk_gla(_flip(sc), _flip(sb), _flip(sx5 * delta[:, :, 1, :, :, None]),
                            _flip(s_loga[:, :, 1]), states[5], with_output)
    new_states = (sa_f, sa_b, sb_f, sb_b, sc_f, sc_b)
    if not with_output:
        return None, None, new_states
    o = {"a_f": oa_f.reshape(b_, seq, -1), "a_b": _flip(oa_b).reshape(b_, seq, -1),
         "b_f": ob_f.reshape(b_, seq, -1), "b_b": _flip(ob_b).reshape(b_, seq, -1),
         "c_f": oc_f.reshape(b_, seq, -1), "c_b": _flip(oc_b).reshape(b_, seq, -1)}
    return o, sx.astype(BF16), new_states


def kernel(x, c, ctx, c_ctx, ada_w, ada_b, ln_g, ln_b, ffn_w13, ffn_w2, mix_w_in,
           gdn_conv_w, gdn_a_log, gdn_dt_bias, gdn_norm_g, ret_decay, ret_norm_g,
           ssd_conv_w, ssd_conv_b, ssd_a_log, ssd_dt_bias, ssd_d, ssd_norm_g,
           w_br_a, w_br_b, w_br_c, mix_w_out):
    depth = ada_w.shape[0]
    alpha = float((2 * depth) ** 0.25)
    b_, seq, d = x.shape
    rope = _axial_rope(seq)
    c_rows = jnp.concatenate([c, c_ctx[None, :]], axis=0)
    h, hc = x, ctx
    for i in range(depth):
        last = i == depth - 1
        mod_all = ada_modulation(c_rows, ada_w[i], ada_b[i].reshape(1, -1))
        mod = mod_all[:b_].reshape(b_, N_ADA, d)
        mod_c = jnp.broadcast_to(mod_all[b_:].reshape(1, N_ADA, d), (b_, N_ADA, d))
        w13 = [ffn_w13[i, j].astype(BF16) for j in range(2)]
        w2 = [ffn_w2[i, j].astype(BF16) for j in range(2)]
        w_in_r = reorder_w_in(mix_w_in[i], d)
        lp = {'gdn_conv_w': gdn_conv_w[i], 'gdn_a_log': gdn_a_log[i],
              'gdn_dt_bias': gdn_dt_bias[i], 'gdn_norm_g': gdn_norm_g[i], 'ret_decay': ret_decay[i],
              'ret_norm_g': ret_norm_g[i], 'ssd_conv_w': ssd_conv_w[i], 'ssd_conv_b': ssd_conv_b[i],
              'ssd_a_log': ssd_a_log[i], 'ssd_dt_bias': ssd_dt_bias[i], 'ssd_d': ssd_d[i],
              'ssd_norm_g': ssd_norm_g[i], 'w_br_a': w_br_a[i].astype(BF16), 'w_br_b': w_br_b[i].astype(BF16),
              'w_br_c': w_br_c[i].astype(BF16), 'w_out': mix_w_out[i].astype(BF16)}
        ffn = functools.partial(ffn_sublayer, alpha=alpha, tm=512)
        h = ffn(h, mod, w13[0], w2[0], ln_g[i, 0], ln_b[i, 0], mod_base=0)
        hc = ffn(hc, mod_c, w13[0], w2[0], ln_g[i, 0], ln_b[i, 0], mod_base=0)
        proj_c = mixer_in_projection(hc, mod_c, w_in_r, tm=256)
        o_c, sx_c, ctx_states = token_mixers(proj_c, _zero_states(b_), None, lp, not last)
        proj = mixer_in_projection(h, mod, w_in_r, tm=256)
        o_l, sx_l, _ = token_mixers(proj, ctx_states, rope, lp, True)
        h = mixer_merge(h, mod, o_l, proj, sx_l, lp, ln_g[i, 1], ln_b[i, 1], alpha=alpha, tm=256)
        h = ffn(h, mod, w13[1], w2[1], ln_g[i, 2], ln_b[i, 2], mod_base=6)
        if not last:
            hc = mixer_merge(hc, mod_c, o_c, proj_c, sx_c, lp, ln_g[i, 1], ln_b[i, 1], alpha=alpha, tm=256)
            hc = ffn(hc, mod_c, w13[1], w2[1], ln_g[i, 2], ln_b[i, 2], mod_base=6)
    return h
```

```python
import functools

import numpy as np
import jax
import jax.numpy as jnp
from jax import lax
from jax.experimental import pallas as pl
from jax.experimental.pallas import tpu as pltpu

F32 = jnp.float32
BF16 = jnp.bfloat16

GRID_W = 64
CONV_K = 5
GDN_HEADS = 4
GDN_DK = 128
GDN_DV = 128
RET_HEADS = 4
RET_DK = 128
RET_DV = 128
SSD_HEADS = 16
SSD_HEADDIM = 64
SSD_GROUPS = 2
SSD_STATE = 128
SSD_DINNER = SSD_HEADS * SSD_HEADDIM
SSD_HG = SSD_HEADS // SSD_GROUPS
N_BRANCH = 3
N_ADA = 9
ROPE_BASE = 10000.0
GDN_QKV = GDN_HEADS * (2 * GDN_DK + GDN_DV)
SSD_XBC = SSD_DINNER + 2 * SSD_GROUPS * SSD_STATE
LN_EPS = 1e-5
RMS_EPS = 1e-6
L2_EPS = 1e-6

VMEM_LIMIT_BYTES = 56 * 1024 * 1024
LANE = 128
SMALL_W = LANE
HALO = 16
SCAN_CHUNK = 128
SCAN_BLOCK = 256

COL_BETA = 0
COL_GLOGA = 2 * GDN_HEADS
COL_DELTA = 4 * GDN_HEADS
COL_SLOGA = COL_DELTA + 2 * SSD_HEADS
COL_END = COL_SLOGA + 2 * SSD_HEADS


def _cparams(n_axes):
    return pltpu.CompilerParams(dimension_semantics=("arbitrary",) * n_axes,
                                vmem_limit_bytes=VMEM_LIMIT_BYTES)


def _resident(shape):
    nd = len(shape)
    return pl.BlockSpec(shape, lambda *_: (0,) * nd, pipeline_mode=pl.Buffered(1))


def _layer_norm_rows(y, g, b):
    mu = jnp.mean(y, axis=-1, keepdims=True)
    yc = y - mu
    var = jnp.mean(yc * yc, axis=-1, keepdims=True)
    return yc * lax.rsqrt(var + LN_EPS) * g + b


def _silu(x):
    return x * jax.nn.sigmoid(x)


def _softplus(x):
    return jnp.maximum(x, 0.0) + jnp.log1p(jnp.exp(-jnp.abs(x)))


def _dot(a, b):
    return jnp.dot(a, b, preferred_element_type=F32)


def _dot_nt(a, b):
    return lax.dot_general(a, b, (((1,), (1,)), ((), ())), preferred_element_type=F32)


def _dot_tn(a, b):
    return lax.dot_general(a, b, (((0,), (0,)), ((), ())), preferred_element_type=F32)


def _ada_body(c_ref, w_ref, b_ref, o_ref):
    s = _silu(c_ref[...]).astype(BF16)
    o_ref[...] = _dot(s, w_ref[...].astype(BF16)) + b_ref[...]


def ada_modulation(c_rows, w, b):
    r, d = c_rows.shape
    n = w.shape[1]
    tn = 1024
    return pl.pallas_call(
        _ada_body,
        grid=(n // tn,),
        in_specs=[pl.BlockSpec((r, d), lambda j: (0, 0)),
                  pl.BlockSpec((d, tn), lambda j: (0, j)),
                  pl.BlockSpec((1, tn), lambda j: (0, j))],
        out_specs=pl.BlockSpec((r, tn), lambda j: (0, j)),
        out_shape=jax.ShapeDtypeStruct((r, n), F32),
        compiler_params=_cparams(1),
        name="ada_modulation",
    )(c_rows, w, b)


def _ffn_body(h_ref, mod_ref, w13_ref, w2_ref, lng_ref, lnb_ref, o_ref, *, alpha, ff, chunks, mod_base):
    h = h_ref[...]
    shift = mod_ref[mod_base:mod_base + 1, :]
    scale = mod_ref[mod_base + 1:mod_base + 2, :]
    gate = mod_ref[mod_base + 2:mod_base + 3, :]
    u = (h * (1.0 + scale) + shift).astype(BF16)
    acc = None
    for c0, c1 in chunks:
        a = _dot(u, w13_ref[:, c0:c1])
        b = _dot(u, w13_ref[:, ff + c0:ff + c1])
        g = (_silu(a) * b).astype(BF16)
        p = _dot(g, w2_ref[c0:c1, :])
        acc = p if acc is None else acc + p
    y = alpha * h + (0.5 * gate) * acc
    o_ref[...] = _layer_norm_rows(y, lng_ref[...], lnb_ref[...])


def _ff_chunks(ff, width=1024):
    return tuple((c0, min(c0 + width, ff)) for c0 in range(0, ff, width))


def ffn_sublayer(h, mod, w13, w2, ln_g, ln_b, *, alpha, mod_base, tm):
    b_, seq, d = h.shape
    ff = w2.shape[0]
    tm = min(tm, seq)
    body = functools.partial(_ffn_body, alpha=alpha, ff=ff, chunks=_ff_chunks(ff), mod_base=mod_base)
    return pl.pallas_call(
        body,
        grid=(b_, seq // tm),
        in_specs=[pl.BlockSpec((None, tm, d), lambda b, i: (b, i, 0)),
                  pl.BlockSpec((None, N_ADA, d), lambda b, i: (b, 0, 0)),
                  _resident(w13.shape), _resident(w2.shape),
                  _resident((1, d)), _resident((1, d))],
        out_specs=pl.BlockSpec((None, tm, d), lambda b, i: (b, i, 0)),
        out_shape=jax.ShapeDtypeStruct(h.shape, F32),
        compiler_params=_cparams(2),
        name="ffn_sublayer",
    )(h, mod, w13, w2, ln_g.reshape(1, d), ln_b.reshape(1, d))


PROJ_GROUPS = (("g_qkv", GDN_QKV, BF16), ("g_z", GDN_HEADS * GDN_DV, BF16),
               ("r_q", RET_HEADS * RET_DK, BF16), ("r_k", RET_HEADS * RET_DK, BF16),
               ("r_v", RET_HEADS * RET_DV, BF16), ("r_g", RET_HEADS * RET_DV, BF16),
               ("s_z", SSD_DINNER, BF16), ("s_xbc", SSD_XBC, BF16),
               ("br_gate", None, BF16), ("small", SMALL_W, F32))


def _proj_groups(d):
    return tuple((n, (N_BRANCH * d if w is None else w), t) for n, w, t in PROJ_GROUPS)


def reorder_w_in(w_in, d):
    sizes = (GDN_QKV, GDN_HEADS * GDN_DV, 2 * GDN_HEADS, 2 * GDN_HEADS,
             RET_HEADS * RET_DK, RET_HEADS * RET_DK, RET_HEADS * RET_DV, RET_HEADS * RET_DV,
             SSD_DINNER, SSD_XBC, 2 * SSD_HEADS, N_BRANCH * d)
    offs = np.concatenate([[0], np.cumsum(sizes)])
    cols = [w_in[:, offs[k]:offs[k + 1]] for k in range(len(sizes))]
    (g_qkv, g_z, g_b, g_a, r_q, r_k, r_v, r_g, s_z, s_xbc, s_dt, br_gate) = cols
    n_small = 4 * GDN_HEADS + 2 * SSD_HEADS
    small = jnp.concatenate([g_b, g_a, s_dt, jnp.zeros((w_in.shape[0], SMALL_W - n_small), w_in.dtype)], axis=1)
    return jnp.concatenate([g_qkv, g_z, r_q, r_k, r_v, r_g, s_z, s_xbc, br_gate, small], axis=1).astype(BF16)


def _inproj_body(h_ref, mod_ref, w_ref, *o_refs, groups):
    shift = mod_ref[3:4, :]
    scale = mod_ref[4:5, :]
    u = (h_ref[...] * (1.0 + scale) + shift).astype(BF16)
    c0 = 0
    for o_ref, (_, width, dtype) in zip(o_refs, groups):
        o_ref[...] = _dot(u, w_ref[:, c0:c0 + width]).astype(dtype)
        c0 += width


def mixer_in_projection(h, mod, w_in_r, *, tm):
    b_, seq, d = h.shape
    groups = _proj_groups(d)
    tm = min(tm, seq)
    body = functools.partial(_inproj_body, groups=groups)
    outs = pl.pallas_call(
        body,
        grid=(b_, seq // tm),
        in_specs=[pl.BlockSpec((None, tm, d), lambda b, i: (b, i, 0)),
                  pl.BlockSpec((None, N_ADA, d), lambda b, i: (b, 0, 0)),
                  _resident(w_in_r.shape)],
        out_specs=[pl.BlockSpec((None, tm, w), lambda b, i: (b, i, 0)) for _, w, _ in groups],
        out_shape=[jax.ShapeDtypeStruct((b_, seq, w), t) for _, w, t in groups],
        compiler_params=_cparams(2),
        name="mixer_in_projection",
    )(h, mod, w_in_r)
    return {n: o for (n, _, _), o in zip(groups, outs)}


def _conv_silu(prev_ref, x_ref, next_ref, w_ref, b_ref, first, last):
    tm = x_ref.shape[0]
    p = jnp.where(first, 0.0, prev_ref[...].astype(F32))
    n = jnp.where(last, 0.0, next_ref[...].astype(F32))
    xx = jnp.concatenate([p, x_ref[...].astype(F32), n], axis=0)
    rows = tm + 2 * HALO
    acc = None
    for k in range(CONV_K):
        s = (CONV_K // 2 - k) % rows
        sh = xx if s == 0 else pltpu.roll(xx, s, axis=0)
        term = sh[HALO:HALO + tm, :] * w_ref[k:k + 1, :]
        acc = term if acc is None else acc + term
    if b_ref is not None:
        acc = acc + b_ref[...]
    return _silu(acc)


def _prep_body(*refs, use_rope):
    (gp_ref, gx_ref, gn_ref, sp_ref, sx_ref, sn_ref, rq_ref, rk_ref, small_ref) = refs[:9]
    k = 9
    if use_rope:
        cos_ref, sin_ref = refs[k:k + 2]
        k += 2
    gw_ref, sw_ref, sb_ref, bias_ref, alog_ref = refs[k:k + 5]
    (gq_o, gk_o, gv_o, rq_o, rk_o, sx_o, sb_o, sc_o, sm_o) = refs[k + 5:]
    i = pl.program_id(1)
    first = i == 0
    last = i == pl.num_programs(1) - 1

    qkv = _conv_silu(gp_ref, gx_ref, gn_ref, gw_ref, None, first, last)
    nq = GDN_HEADS * GDN_DK
    for hh in range(GDN_HEADS):
        sl = slice(hh * GDN_DK, (hh + 1) * GDN_DK)
        q = qkv[:, sl]
        kk = qkv[:, nq + hh * GDN_DK: nq + (hh + 1) * GDN_DK]
        q = q * lax.rsqrt(jnp.sum(q * q, axis=-1, keepdims=True) + L2_EPS) * (GDN_DK ** -0.5)
        kk = kk * lax.rsqrt(jnp.sum(kk * kk, axis=-1, keepdims=True) + L2_EPS)
        gq_o[:, sl] = q.astype(BF16)
        gk_o[:, sl] = kk.astype(BF16)
    gv_o[...] = qkv[:, 2 * nq:].astype(BF16)

    rq = rq_ref[...].astype(F32)
    rk = rk_ref[...].astype(F32) * (RET_DK ** -0.5)
    if use_rope:
        cos2 = cos_ref[...]
        sin2 = sin_ref[...]
        for hh in range(RET_HEADS):
            sl = slice(hh * RET_DK, (hh + 1) * RET_DK)
            for src, dst in ((rq, rq_o), (rk, rk_o)):
                xh = src[:, sl]
                dst[:, sl] = (xh * cos2 + pltpu.roll(xh, RET_DK // 2, axis=1) * sin2).astype(BF16)
    else:
        rq_o[...] = rq.astype(BF16)
        rk_o[...] = rk.astype(BF16)

    xbc = _conv_silu(sp_ref, sx_ref, sn_ref, sw_ref, sb_ref, first, last)
    sx_o[...] = xbc[:, :SSD_DINNER].astype(BF16)
    sb_o[...] = xbc[:, SSD_DINNER:SSD_DINNER + SSD_GROUPS * SSD_STATE].astype(BF16)
    sc_o[...] = xbc[:, SSD_DINNER + SSD_GROUPS * SSD_STATE:].astype(BF16)

    sm = small_ref[...]
    col = lax.broadcasted_iota(jnp.int32, sm.shape, 1)
    shifted = pltpu.roll(sm, COL_SLOGA - COL_DELTA, axis=1)
    xin = jnp.where(col < COL_SLOGA, sm, shifted)
    sp = _softplus(xin + bias_ref[...])
    neg_a = -jnp.exp(alog_ref[...])
    out = jnp.where(col < COL_GLOGA, jax.nn.sigmoid(sm),
                    jnp.where(col < COL_DELTA, neg_a * sp,
                              jnp.where(col < COL_SLOGA, sp,
                                        jnp.where(col < COL_END, sp * neg_a, 0.0))))
    sm_o[...] = out


def mixer_prep(proj, rope2, lp, *, tm):
    g_qkv = proj["g_qkv"]
    b_, seq, _ = g_qkv.shape
    tm = min(tm, seq)
    nh = tm // HALO
    n_halo = seq // HALO
    use_rope = rope2 is not None

    def tok(w):
        return pl.BlockSpec((None, tm, w), lambda b, i: (b, i, 0))

    def prev(w):
        return pl.BlockSpec((None, HALO, w), lambda b, i: (b, jnp.maximum(i * nh - 1, 0), 0))

    def nxt(w):
        return pl.BlockSpec((None, HALO, w), lambda b, i: (b, jnp.minimum((i + 1) * nh, n_halo - 1), 0))

    def row(w):
        return pl.BlockSpec((tm, w), lambda b, i: (i, 0))

    zeros = lambda n: jnp.zeros((n,), F32)
    bias_row = jnp.concatenate([zeros(COL_GLOGA), lp["gdn_dt_bias"].astype(F32).reshape(-1),
                                lp["ssd_dt_bias"].astype(F32).reshape(-1),
                                lp["ssd_dt_bias"].astype(F32).reshape(-1),
                                zeros(SMALL_W - COL_END)]).reshape(1, SMALL_W)
    alog_row = jnp.concatenate([zeros(COL_GLOGA), lp["gdn_a_log"].astype(F32).reshape(-1),
                                zeros(COL_SLOGA - COL_DELTA), lp["ssd_a_log"].astype(F32).reshape(-1),
                                zeros(SMALL_W - COL_END)]).reshape(1, SMALL_W)
    args = [g_qkv, g_qkv, g_qkv, proj["s_xbc"], proj["s_xbc"], proj["s_xbc"],
            proj["r_q"], proj["r_k"], proj["small"]]
    in_specs = [prev(GDN_QKV), tok(GDN_QKV), nxt(GDN_QKV), prev(SSD_XBC), tok(SSD_XBC), nxt(SSD_XBC),
                tok(RET_HEADS * RET_DK), tok(RET_HEADS * RET_DK), tok(SMALL_W)]
    if use_rope:
        args += [rope2[0], rope2[1]]
        in_specs += [row(RET_DK), row(RET_DK)]
    consts = [lp["gdn_conv_w"].astype(F32), lp["ssd_conv_w"].astype(F32),
              lp["ssd_conv_b"].astype(F32).reshape(1, SSD_XBC), bias_row, alog_row]
    args += consts
    in_specs += [_resident(a.shape) for a in consts]
    names = ("gq", "gk", "gv", "rq", "rk", "sx", "sb", "sc", "smallp")
    widths = (GDN_HEADS * GDN_DK, GDN_HEADS * GDN_DK, GDN_HEADS * GDN_DV, RET_HEADS * RET_DK,
              RET_HEADS * RET_DK, SSD_DINNER, SSD_GROUPS * SSD_STATE, SSD_GROUPS * SSD_STATE, SMALL_W)
    dtypes = (BF16,) * 8 + (F32,)
    outs = pl.pallas_call(
        functools.partial(_prep_body, use_rope=use_rope),
        grid=(b_, seq // tm),
        in_specs=in_specs,
        out_specs=[tok(w) for w in widths],
        out_shape=[jax.ShapeDtypeStruct((b_, seq, w), t) for w, t in zip(widths, dtypes)],
        compiler_params=_cparams(2),
        name="mixer_prep",
    )(*args)
    return dict(zip(names, outs))


def rope_tables(seq_len):
    rows = seq_len // GRID_W
    row_id = jnp.repeat(jnp.arange(rows, dtype=F32), GRID_W)
    col_id = jnp.tile(jnp.arange(GRID_W, dtype=F32), rows)
    n_freq = RET_DK // 4
    inv_freq = ROPE_BASE ** (-jnp.arange(n_freq, dtype=F32) / n_freq)
    ang = jnp.concatenate([row_id[:, None] * inv_freq, col_id[:, None] * inv_freq], axis=-1)
    cos, sin = jnp.cos(ang), jnp.sin(ang)
    return jnp.concatenate([cos, cos], axis=-1), jnp.concatenate([-sin, sin], axis=-1)


def _tri_masks(c, reverse):
    r = lax.broadcasted_iota(jnp.int32, (c, c), 0)
    col = lax.broadcasted_iota(jnp.int32, (c, c), 1)
    if reverse:
        return col > r, col >= r
    return col < r, col <= r


def _split3(x):
    hi = x.astype(BF16)
    r1 = x - hi.astype(F32)
    mid = r1.astype(BF16)
    lo = (r1 - mid.astype(F32)).astype(BF16)
    return hi, mid, lo


def _chunk_sums(x, c, reverse):
    t = x.shape[0]
    shift = int(np.log2(c))
    r = lax.broadcasted_iota(jnp.int32, (t, t), 0)
    col = lax.broadcasted_iota(jnp.int32, (t, t), 1)
    same = lax.shift_right_logical(r, shift) == lax.shift_right_logical(col, shift)
    tri = (col >= r) if reverse else (col <= r)
    m_tot = jnp.where(same, 1.0, 0.0).astype(BF16)
    m_cum = jnp.where(same, jnp.where(tri, 1.0, 0.0), 0.0).astype(BF16)
    parts = _split3(x)
    cum = _dot(m_cum, parts[0]) + _dot(m_cum, parts[1]) + _dot(m_cum, parts[2])
    tot = _dot(m_tot, parts[0]) + _dot(m_tot, parts[1]) + _dot(m_tot, parts[2])
    return cum, tot


def _decay_matrix(gcol, grow, incl):
    return jnp.where(incl, jnp.exp(jnp.where(incl, gcol - grow, 0.0)), 0.0)


def _merge_level_masks(c):
    r = lax.broadcasted_iota(jnp.int32, (c, c), 0)
    col = lax.broadcasted_iota(jnp.int32, (c, c), 1)
    x = lax.bitwise_xor(r, col)
    return [lax.shift_right_logical(x, k) == 1 for k in range(int(np.log2(c)))]


def _unit_triangular_inverse_minus_eye(a, levels):
    m = -jnp.where(levels[0], a, 0.0)
    eye = jnp.where(lax.broadcasted_iota(jnp.int32, a.shape, 0) == lax.broadcasted_iota(jnp.int32, a.shape, 1),
                    1.0, 0.0)
    for lvl in levels[1:]:
        tb = (m + eye).astype(BF16)
        e = jnp.where(lvl, a, 0.0).astype(BF16)
        m = m - _dot(tb, _dot(e, tb).astype(BF16))
    return m


def _chunk_order(n, reverse):
    return range(n - 1, -1, -1) if reverse else range(n)


def _gdn_direction(d, q_ref, k_ref, v_ref, p_ref, o_ref, st_ref, *, c, with_output):
    reverse = d == 1
    t = q_ref.shape[0]
    gc, tot = _chunk_sums(p_ref[...], c, reverse)
    strict, incl = _tri_masks(c, reverse)
    levels = _merge_level_masks(c)
    sp = p_ref[...]
    for ci in _chunk_order(t // c, reverse):
        rows = slice(ci * c, (ci + 1) * c)
        gcc = gc[rows, :]
        gct = gcc.T
        totc = tot[rows, :]
        for hh in range(GDN_HEADS):
            cb = COL_BETA + d * GDN_HEADS + hh
            cl = COL_GLOGA + d * GDN_HEADS + hh
            hs = slice(hh * GDN_DK, (hh + 1) * GDN_DK)
            kb = k_ref[rows, hs]
            kf = kb.astype(F32)
            vf = v_ref[rows, hs].astype(F32)
            gcol = gcc[:, cl:cl + 1]
            grow = gct[cl:cl + 1, :]
            tcol = totc[:, cl:cl + 1]
            beta = sp[rows, cb:cb + 1]
            e_incl = _decay_matrix(gcol, grow, incl)
            e_strict = jnp.where(strict, e_incl, 0.0)
            a_mat = _dot_nt(kb, kb) * e_strict * beta
            rhs = jnp.concatenate([beta * vf, (beta * jnp.exp(gcol)) * kf], axis=1)
            x = rhs + _dot(_unit_triangular_inverse_minus_eye(a_mat, levels).astype(BF16), rhs.astype(BF16))
            w_v = x[:, :GDN_DV]
            w_k = x[:, GDN_DV:]
            s = st_ref[d, hh]
            sb = s.astype(BF16)
            if with_output:
                qb = q_ref[rows, hs]
                q_w = (qb.astype(F32) * jnp.exp(gcol)).astype(BF16)
                ws = _dot(jnp.concatenate([w_k.astype(BF16), q_w], axis=0), sb)
                u = w_v - ws[:c]
                p_mat = (_dot_nt(qb, kb) * e_incl).astype(BF16)
                o_ref[rows, hs] = ws[c:] + _dot(p_mat, u.astype(BF16))
            else:
                u = w_v - _dot(w_k.astype(BF16), sb)
            k_w = (kf * jnp.exp(tcol - gcol)).astype(BF16)
            st_ref[d, hh] = jnp.exp(tcol[0:1, :]) * s + _dot_tn(k_w, u.astype(BF16))


def _gdn_body(*refs, c, with_output):
    s0_ref = refs[0]
    ins = refs[1:9]
    if with_output:
        outs = refs[9:11]
        st_ref = refs[11]
    else:
        outs = (None, None)
        st_ref = refs[9]

    @pl.when(pl.program_id(1) == 0)
    def _():
        st_ref[...] = s0_ref[...]

    for d in range(2):
        q_ref, k_ref, v_ref, p_ref = ins[4 * d:4 * d + 4]
        _gdn_direction(d, q_ref, k_ref, v_ref, p_ref, outs[d], st_ref, c=c, with_output=with_output)


def _ret_direction(d, q_ref, k_ref, v_ref, lg_ref, o_ref, st_ref, *, c, with_output):
    reverse = d == 1
    t = q_ref.shape[0]
    r = lax.broadcasted_iota(jnp.int32, (c, c), 0)
    col = lax.broadcasted_iota(jnp.int32, (c, c), 1)
    dist = ((col - r) if reverse else (r - col)).astype(F32)
    incl = dist >= 0.0
    pos = lax.broadcasted_iota(jnp.int32, (c, 1), 0).astype(F32)
    steps = (float(c) - pos) if reverse else (pos + 1.0)
    for ci in _chunk_order(t // c, reverse):
        rows = slice(ci * c, (ci + 1) * c)
        for hh in range(RET_HEADS):
            hs = slice(hh * RET_DK, (hh + 1) * RET_DK)
            lg = lg_ref[d, hh]
            kb = k_ref[rows, hs]
            vf = v_ref[rows, hs].astype(F32)
            s = st_ref[d, hh]
            if with_output:
                qb = q_ref[rows, hs]
                dmat = jnp.where(incl, jnp.exp(jnp.where(incl, dist * lg, 0.0)), 0.0)
                intra = _dot((_dot_nt(qb, kb) * dmat).astype(BF16), vf.astype(BF16))
                o_ref[rows, hs] = intra + _dot(qb, s.astype(BF16)) * jnp.exp(steps * lg)
            v_w = (vf * jnp.exp((float(c) - steps) * lg)).astype(BF16)
            c_dec = jnp.exp(jnp.full((1, 1), float(c), F32) * lg)
            st_ref[d, hh] = c_dec * s + _dot_tn(kb, v_w)


def _ret_body(*refs, c, with_output):
    lg_ref, s0_ref = refs[0], refs[1]
    ins = refs[2:8]
    if with_output:
        outs = refs[8:10]
        st_ref = refs[10]
    else:
        outs = (None, None)
        st_ref = refs[8]

    @pl.when(pl.program_id(1) == 0)
    def _():
        st_ref[...] = s0_ref[...]

    for d in range(2):
        q_ref, k_ref, v_ref = ins[3 * d:3 * d + 3]
        _ret_direction(d, q_ref, k_ref, v_ref, lg_ref, outs[d], st_ref, c=c, with_output=with_output)


def _ssd_direction(d, x_ref, b_ref, c_ref, p_ref, o_ref, st_ref, *, c, with_output):
    reverse = d == 1
    t = x_ref.shape[0]
    sp = p_ref[...]
    gc, tot = _chunk_sums(sp, c, reverse)
    _, incl = _tri_masks(c, reverse)
    for ci in _chunk_order(t // c, reverse):
        rows = slice(ci * c, (ci + 1) * c)
        gcc = gc[rows, :]
        gct = gcc.T
        totc = tot[rows, :]
        for gg in range(SSD_GROUPS):
            gs = slice(gg * SSD_STATE, (gg + 1) * SSD_STATE)
            bb = b_ref[rows, gs]
            s = st_ref[d, gg]
            if with_output:
                cb = c_ref[rows, gs]
                scores = _dot_nt(cb, bb)
                inter = _dot(cb, s.astype(BF16))
            vws = []
            cdecs = []
            for hl in range(SSD_HG):
                hd = gg * SSD_HG + hl
                cl = COL_SLOGA + d * SSD_HEADS + hd
                cd = COL_DELTA + d * SSD_HEADS + hd
                xs = slice(hd * SSD_HEADDIM, (hd + 1) * SSD_HEADDIM)
                gcol = gcc[:, cl:cl + 1]
                tcol = totc[:, cl:cl + 1]
                xd = x_ref[rows, xs].astype(F32) * sp[rows, cd:cd + 1]
                if with_output:
                    dmat = _decay_matrix(gcol, gct[cl:cl + 1, :], incl)
                    intra = _dot((scores * dmat).astype(BF16), xd.astype(BF16))
                    o_ref[rows, xs] = intra + inter[:, hl * SSD_HEADDIM:(hl + 1) * SSD_HEADDIM] * jnp.exp(gcol)
                vws.append((xd * jnp.exp(tcol - gcol)).astype(BF16))
                cdecs.append(jnp.broadcast_to(jnp.exp(tcol[0:1, :]), (1, SSD_HEADDIM)))
            v_w = jnp.concatenate(vws, axis=1)
            c_dec = jnp.concatenate(cdecs, axis=1)
            st_ref[d, gg] = c_dec * s + _dot_tn(bb, v_w)


def _ssd_body(*refs, c, with_output):
    s0_ref = refs[0]
    ins = refs[1:9]
    if with_output:
        outs = refs[9:11]
        st_ref = refs[11]
    else:
        outs = (None, None)
        st_ref = refs[9]

    @pl.when(pl.program_id(1) == 0)
    def _():
        st_ref[...] = s0_ref[...]

    for d in range(2):
        x_ref, b_ref, c_ref, p_ref = ins[4 * d:4 * d + 4]
        _ssd_direction(d, x_ref, b_ref, c_ref, p_ref, outs[d], st_ref, c=c, with_output=with_output)


def _scan_call(body, name, s0, arrays, out_width, with_output, extra_args=(), extra_specs=()):
    b_, seq, _ = arrays[0].shape
    t = min(SCAN_BLOCK, seq)
    c = min(SCAN_CHUNK, t)
    nt = seq // t
    fwd = lambda w: pl.BlockSpec((None, t, w), lambda b, i: (b, i, 0))
    bwd = lambda w: pl.BlockSpec((None, t, w), lambda b, i: (b, nt - 1 - i, 0))
    st_spec = pl.BlockSpec((None,) + s0.shape[1:], lambda b, i: (b,) + (0,) * (s0.ndim - 1))
    in_specs = list(extra_specs) + [st_spec]
    in_specs += [fwd(a.shape[-1]) for a in arrays] + [bwd(a.shape[-1]) for a in arrays]
    out_specs, out_shape = [], []
    if with_output:
        out_specs += [fwd(out_width), bwd(out_width)]
        out_shape += [jax.ShapeDtypeStruct((b_, seq, out_width), F32)] * 2
    out_specs.append(st_spec)
    out_shape.append(jax.ShapeDtypeStruct(s0.shape, F32))
    res = pl.pallas_call(
        functools.partial(body, c=c, with_output=with_output),
        grid=(b_, nt),
        in_specs=in_specs,
        out_specs=out_specs,
        out_shape=out_shape,
        compiler_params=_cparams(2),
        name=name,
    )(*extra_args, s0, *arrays, *arrays)
    if with_output:
        return res[0], res[1], res[2]
    return None, None, res[0]


def token_mixers(proj, prep, states, lp, with_output):
    gdn_s0, ret_s0, ssd_s0 = states
    a_f, a_b, gdn_s = _scan_call(_gdn_body, "gdn_scan", gdn_s0,
                                 [prep["gq"], prep["gk"], prep["gv"], prep["smallp"]],
                                 GDN_HEADS * GDN_DV, with_output)
    ret_lg = -jnp.exp(lp["ret_decay"].astype(F32))
    b_f, b_b, ret_s = _scan_call(_ret_body, "ret_scan", ret_s0,
                                 [prep["rq"], prep["rk"], proj["r_v"]],
                                 RET_HEADS * RET_DV, with_output, extra_args=(ret_lg,),
                                 extra_specs=(pl.BlockSpec(memory_space=pltpu.SMEM),))
    c_f, c_b, ssd_s = _scan_call(_ssd_body, "ssd_scan", ssd_s0,
                                 [prep["sx"], prep["sb"], prep["sc"], prep["smallp"]],
                                 SSD_DINNER, with_output)
    o = None
    if with_output:
        o = {"a_f": a_f, "a_b": a_b, "b_f": b_f, "b_b": b_b, "c_f": c_f, "c_b": c_b}
    return o, (gdn_s, ret_s, ssd_s)


def zero_states(b_):
    return (jnp.zeros((b_, 2, GDN_HEADS, GDN_DK, GDN_DV), F32),
            jnp.zeros((b_, 2, RET_HEADS, RET_DK, RET_DV), F32),
            jnp.zeros((b_, 2, SSD_GROUPS, SSD_STATE, SSD_HG * SSD_HEADDIM), F32))


def _merge_body(h_ref, mod_ref, oaf_ref, oab_ref, obf_ref, obb_ref, ocf_ref, ocb_ref, sx_ref,
                gz_ref, rg_ref, sz_ref, brg_ref, gng_ref, rng_ref, sd_ref, sng_ref,
                wa_ref, wb_ref, wc_ref, wo_ref, lng_ref, lnb_ref, o_ref, *, alpha):
    d = h_ref.shape[-1]
    oa = oaf_ref[...] + oab_ref[...]
    gz = gz_ref[...].astype(F32)
    ya = []
    for hh in range(GDN_HEADS):
        x = oa[:, hh * GDN_DV:(hh + 1) * GDN_DV]
        ms = jnp.mean(x * x, axis=-1, keepdims=True)
        ya.append(x * lax.rsqrt(ms + RMS_EPS) * gng_ref[...] * _silu(gz[:, hh * GDN_DV:(hh + 1) * GDN_DV]))
    ya = jnp.concatenate(ya, axis=-1).astype(BF16)
    ob = obf_ref[...] + obb_ref[...]
    rg = rg_ref[...].astype(F32)
    yb = []
    for hh in range(RET_HEADS):
        sl = slice(hh * RET_DV, (hh + 1) * RET_DV)
        x = ob[:, sl]
        mu = jnp.mean(x, axis=-1, keepdims=True)
        xc = x - mu
        var = jnp.mean(xc * xc, axis=-1, keepdims=True)
        yb.append(xc * lax.rsqrt(var + LN_EPS) * rng_ref[:, sl] * _silu(rg[:, sl]))
    yb = jnp.concatenate(yb, axis=-1).astype(BF16)
    oc = (ocf_ref[...] + ocb_ref[...] + sd_ref[...] * sx_ref[...].astype(F32)) * _silu(sz_ref[...].astype(F32))
    gw = SSD_DINNER // SSD_GROUPS
    yc = []
    for gg in range(SSD_GROUPS):
        sl = slice(gg * gw, (gg + 1) * gw)
        x = oc[:, sl]
        ms = jnp.mean(x * x, axis=-1, keepdims=True)
        yc.append(x * lax.rsqrt(ms + RMS_EPS) * sng_ref[:, sl])
    yc = jnp.concatenate(yc, axis=-1).astype(BF16)
    gates = jax.nn.sigmoid(brg_ref[...].astype(F32))
    merged = (gates[:, 0:d] * _dot(ya, wa_ref[...])
              + gates[:, d:2 * d] * _dot(yb, wb_ref[...])
              + gates[:, 2 * d:3 * d] * _dot(yc, wc_ref[...]))
    mix = _dot(merged.astype(BF16), wo_ref[...])
    h = h_ref[...]
    y = alpha * h + mod_ref[5:6, :] * mix
    o_ref[...] = _layer_norm_rows(y, lng_ref[...], lnb_ref[...])


def mixer_merge(h, mod, o, proj, sx, lp, ln_g, ln_b, *, alpha, tm):
    b_, seq, d = h.shape
    tm = min(tm, seq)

    def tok(w):
        return pl.BlockSpec((None, tm, w), lambda b, i: (b, i, 0))

    ssd_d_cols = jnp.repeat(lp["ssd_d"].astype(F32), SSD_HEADDIM).reshape(1, SSD_DINNER)
    args = [h, mod, o["a_f"], o["a_b"], o["b_f"], o["b_b"], o["c_f"], o["c_b"], sx,
            proj["g_z"], proj["r_g"], proj["s_z"], proj["br_gate"],
            lp["gdn_norm_g"].astype(F32).reshape(1, GDN_DV),
            lp["ret_norm_g"].astype(F32).reshape(1, RET_HEADS * RET_DV),
            ssd_d_cols, lp["ssd_norm_g"].astype(F32).reshape(1, SSD_DINNER),
            lp["w_br_a"], lp["w_br_b"], lp["w_br_c"], lp["w_out"], ln_g.reshape(1, d), ln_b.reshape(1, d)]
    in_specs = [tok(d), pl.BlockSpec((None, N_ADA, d), lambda b, i: (b, 0, 0)),
                tok(GDN_HEADS * GDN_DV), tok(GDN_HEADS * GDN_DV),
                tok(RET_HEADS * RET_DV), tok(RET_HEADS * RET_DV),
                tok(SSD_DINNER), tok(SSD_DINNER), tok(SSD_DINNER),
                tok(GDN_HEADS * GDN_DV), tok(RET_HEADS * RET_DV), tok(SSD_DINNER), tok(N_BRANCH * d)]
    in_specs += [_resident(a.shape) for a in args[13:]]
    return pl.pallas_call(
        functools.partial(_merge_body, alpha=alpha),
        grid=(b_, seq // tm),
        in_specs=in_specs,
        out_specs=tok(d),
        out_shape=jax.ShapeDtypeStruct(h.shape, F32),
        compiler_params=_cparams(2),
        name="mixer_merge",
    )(*args)


def kernel(x, c, ctx, c_ctx, ada_w, ada_b, ln_g, ln_b, ffn_w13, ffn_w2, mix_w_in,
           gdn_conv_w, gdn_a_log, gdn_dt_bias, gdn_norm_g, ret_decay, ret_norm_g,
           ssd_conv_w, ssd_conv_b, ssd_a_log, ssd_dt_bias, ssd_d, ssd_norm_g,
           w_br_a, w_br_b, w_br_c, mix_w_out):
    depth = ada_w.shape[0]
    alpha = float((2 * depth) ** 0.25)
    b_, seq, d = x.shape
    rope2 = rope_tables(seq)
    c_rows = jnp.concatenate([c, c_ctx[None, :]], axis=0)
    h, hc = x, ctx
    for i in range(depth):
        last = i == depth - 1
        mod_all = ada_modulation(c_rows, ada_w[i], ada_b[i].reshape(1, -1))
        mod = mod_all[:b_].reshape(b_, N_ADA, d)
        mod_c = jnp.broadcast_to(mod_all[b_:].reshape(1, N_ADA, d), (b_, N_ADA, d))
        w13 = [ffn_w13[i, j].astype(BF16) for j in range(2)]
        w2 = [ffn_w2[i, j].astype(BF16) for j in range(2)]
        w_in_r = reorder_w_in(mix_w_in[i], d)
        lp = {'gdn_conv_w': gdn_conv_w[i], 'gdn_a_log': gdn_a_log[i],
              'gdn_dt_bias': gdn_dt_bias[i], 'gdn_norm_g': gdn_norm_g[i], 'ret_decay': ret_decay[i],
              'ret_norm_g': ret_norm_g[i], 'ssd_conv_w': ssd_conv_w[i], 'ssd_conv_b': ssd_conv_b[i],
              'ssd_a_log': ssd_a_log[i], 'ssd_dt_bias': ssd_dt_bias[i], 'ssd_d': ssd_d[i],
              'ssd_norm_g': ssd_norm_g[i], 'w_br_a': w_br_a[i].astype(BF16), 'w_br_b': w_br_b[i].astype(BF16),
              'w_br_c': w_br_c[i].astype(BF16), 'w_out': mix_w_out[i].astype(BF16)}
        ffn = functools.partial(ffn_sublayer, alpha=alpha, tm=512)
        h = ffn(h, mod, w13[0], w2[0], ln_g[i, 0], ln_b[i, 0], mod_base=0)
        hc = ffn(hc, mod_c, w13[0], w2[0], ln_g[i, 0], ln_b[i, 0], mod_base=0)
        proj_c = mixer_in_projection(hc, mod_c, w_in_r, tm=256)
        prep_c = mixer_prep(proj_c, None, lp, tm=256)
        o_c, ctx_states = token_mixers(proj_c, prep_c, zero_states(b_), lp, not last)
        proj = mixer_in_projection(h, mod, w_in_r, tm=256)
        prep = mixer_prep(proj, rope2, lp, tm=256)
        o_l, _ = token_mixers(proj, prep, ctx_states, lp, True)
        h = mixer_merge(h, mod, o_l, proj, prep["sx"], lp, ln_g[i, 1], ln_b[i, 1], alpha=alpha, tm=256)
        h = ffn(h, mod, w13[1], w2[1], ln_g[i, 2], ln_b[i, 2], mod_base=6)
        if not last:
            hc = mixer_merge(hc, mod_c, o_c, proj_c, prep_c["sx"], lp, ln_g[i, 1], ln_b[i, 1],
                             alpha=alpha, tm=256)
            hc = ffn(hc, mod_c, w13[1], w2[1], ln_g[i, 2], ln_b[i, 2], mod_base=6)
    return h
```

```python
import functools

import numpy as np
import jax
import jax.numpy as jnp
from jax import lax
from jax.experimental import pallas as pl
from jax.experimental.pallas import tpu as pltpu

F32 = jnp.float32
BF16 = jnp.bfloat16

GRID_W = 64
CONV_K = 5
GDN_HEADS = 4
GDN_DK = 128
GDN_DV = 128
RET_HEADS = 4
RET_DK = 128
RET_DV = 128
SSD_HEADS = 16
SSD_HEADDIM = 64
SSD_GROUPS = 2
SSD_STATE = 128
SSD_DINNER = SSD_HEADS * SSD_HEADDIM
SSD_HG = SSD_HEADS // SSD_GROUPS
N_BRANCH = 3
N_ADA = 9
ROPE_BASE = 10000.0
GDN_QKV = GDN_HEADS * (2 * GDN_DK + GDN_DV)
SSD_XBC = SSD_DINNER + 2 * SSD_GROUPS * SSD_STATE
LN_EPS = 1e-5
RMS_EPS = 1e-6
L2_EPS = 1e-6

VMEM_LIMIT_BYTES = 56 * 1024 * 1024
LANE = 128
SMALL_W = LANE
HALO = 16
CONV_ROWS = 128
SCAN_CHUNK = 128
SCAN_BLOCK = 256

COL_BETA = 0
COL_GLOGA = 2 * GDN_HEADS
COL_DELTA = 4 * GDN_HEADS
COL_SLOGA = COL_DELTA + 2 * SSD_HEADS
COL_END = COL_SLOGA + 2 * SSD_HEADS


def _cparams(n_axes):
    return pltpu.CompilerParams(dimension_semantics=("arbitrary",) * n_axes,
                                vmem_limit_bytes=VMEM_LIMIT_BYTES)


def _resident(shape):
    nd = len(shape)
    return pl.BlockSpec(shape, lambda *_: (0,) * nd, pipeline_mode=pl.Buffered(1))


def _layer_norm_rows(y, g, b):
    mu = jnp.mean(y, axis=-1, keepdims=True)
    yc = y - mu
    var = jnp.mean(yc * yc, axis=-1, keepdims=True)
    return yc * lax.rsqrt(var + LN_EPS) * g + b


def _silu(x):
    return x * jax.nn.sigmoid(x)


def _softplus(x):
    return jnp.maximum(x, 0.0) + jnp.log1p(jnp.exp(-jnp.abs(x)))


def _dot(a, b):
    return jnp.dot(a, b, preferred_element_type=F32)


def _dot_nt(a, b):
    return lax.dot_general(a, b, (((1,), (1,)), ((), ())), preferred_element_type=F32)


def _dot_tn(a, b):
    return lax.dot_general(a, b, (((0,), (0,)), ((), ())), preferred_element_type=F32)


def _ada_body(c_ref, w_ref, b_ref, o_ref):
    s = _silu(c_ref[...]).astype(BF16)
    o_ref[...] = _dot(s, w_ref[...].astype(BF16)) + b_ref[...]


def ada_modulation(c_rows, w, b):
    r, d = c_rows.shape
    n = w.shape[1]
    tn = 1024
    return pl.pallas_call(
        _ada_body,
        grid=(n // tn,),
        in_specs=[pl.BlockSpec((r, d), lambda j: (0, 0)),
                  pl.BlockSpec((d, tn), lambda j: (0, j)),
                  pl.BlockSpec((1, tn), lambda j: (0, j))],
        out_specs=pl.BlockSpec((r, tn), lambda j: (0, j)),
        out_shape=jax.ShapeDtypeStruct((r, n), F32),
        compiler_params=_cparams(1),
        name="ada_modulation",
    )(c_rows, w, b)


def _ffn_body(h_ref, mod_ref, w13_ref, w2_ref, lng_ref, lnb_ref, o_ref, *, alpha, ff, chunks, mod_base):
    h = h_ref[...]
    shift = mod_ref[mod_base:mod_base + 1, :]
    scale = mod_ref[mod_base + 1:mod_base + 2, :]
    gate = mod_ref[mod_base + 2:mod_base + 3, :]
    u = (h * (1.0 + scale) + shift).astype(BF16)
    acc = None
    for c0, c1 in chunks:
        a = _dot(u, w13_ref[:, c0:c1])
        b = _dot(u, w13_ref[:, ff + c0:ff + c1])
        g = (_silu(a) * b).astype(BF16)
        p = _dot(g, w2_ref[c0:c1, :])
        acc = p if acc is None else acc + p
    y = alpha * h + (0.5 * gate) * acc
    o_ref[...] = _layer_norm_rows(y, lng_ref[...], lnb_ref[...])


def _ff_chunks(ff, width=1024):
    return tuple((c0, min(c0 + width, ff)) for c0 in range(0, ff, width))


def ffn_sublayer(h, mod, w13, w2, ln_g, ln_b, *, alpha, mod_base, tm):
    b_, seq, d = h.shape
    ff = w2.shape[0]
    tm = min(tm, seq)
    body = functools.partial(_ffn_body, alpha=alpha, ff=ff, chunks=_ff_chunks(ff), mod_base=mod_base)
    return pl.pallas_call(
        body,
        grid=(b_, seq // tm),
        in_specs=[pl.BlockSpec((None, tm, d), lambda b, i: (b, i, 0)),
                  pl.BlockSpec((None, N_ADA, d), lambda b, i: (b, 0, 0)),
                  _resident(w13.shape), _resident(w2.shape),
                  _resident((1, d)), _resident((1, d))],
        out_specs=pl.BlockSpec((None, tm, d), lambda b, i: (b, i, 0)),
        out_shape=jax.ShapeDtypeStruct(h.shape, F32),
        compiler_params=_cparams(2),
        name="ffn_sublayer",
    )(h, mod, w13, w2, ln_g.reshape(1, d), ln_b.reshape(1, d))


PROJ_GROUPS = (("g_qkv", GDN_QKV, BF16), ("g_z", GDN_HEADS * GDN_DV, BF16),
               ("r_q", RET_HEADS * RET_DK, BF16), ("r_k", RET_HEADS * RET_DK, BF16),
               ("r_v", RET_HEADS * RET_DV, BF16), ("r_g", RET_HEADS * RET_DV, BF16),
               ("s_z", SSD_DINNER, BF16), ("s_xbc", SSD_XBC, BF16),
               ("br_gate", None, BF16), ("small", SMALL_W, F32))


def _proj_groups(d):
    return tuple((n, (N_BRANCH * d if w is None else w), t) for n, w, t in PROJ_GROUPS)


def reorder_w_in(w_in, d):
    sizes = (GDN_QKV, GDN_HEADS * GDN_DV, 2 * GDN_HEADS, 2 * GDN_HEADS,
             RET_HEADS * RET_DK, RET_HEADS * RET_DK, RET_HEADS * RET_DV, RET_HEADS * RET_DV,
             SSD_DINNER, SSD_XBC, 2 * SSD_HEADS, N_BRANCH * d)
    offs = np.concatenate([[0], np.cumsum(sizes)])
    cols = [w_in[:, offs[k]:offs[k + 1]] for k in range(len(sizes))]
    (g_qkv, g_z, g_b, g_a, r_q, r_k, r_v, r_g, s_z, s_xbc, s_dt, br_gate) = cols
    n_small = 4 * GDN_HEADS + 2 * SSD_HEADS
    small = jnp.concatenate([g_b, g_a, s_dt, jnp.zeros((w_in.shape[0], SMALL_W - n_small), w_in.dtype)], axis=1)
    return jnp.concatenate([g_qkv, g_z, r_q, r_k, r_v, r_g, s_z, s_xbc, br_gate, small], axis=1).astype(BF16)


def _inproj_body(h_ref, mod_ref, w_ref, *o_refs, groups):
    shift = mod_ref[3:4, :]
    scale = mod_ref[4:5, :]
    u = (h_ref[...] * (1.0 + scale) + shift).astype(BF16)
    c0 = 0
    for o_ref, (_, width, dtype) in zip(o_refs, groups):
        o_ref[...] = _dot(u, w_ref[:, c0:c0 + width]).astype(dtype)
        c0 += width


def mixer_in_projection(h, mod, w_in_r, *, tm):
    b_, seq, d = h.shape
    groups = _proj_groups(d)
    tm = min(tm, seq)
    body = functools.partial(_inproj_body, groups=groups)
    outs = pl.pallas_call(
        body,
        grid=(b_, seq // tm),
        in_specs=[pl.BlockSpec((None, tm, d), lambda b, i: (b, i, 0)),
                  pl.BlockSpec((None, N_ADA, d), lambda b, i: (b, 0, 0)),
                  _resident(w_in_r.shape)],
        out_specs=[pl.BlockSpec((None, tm, w), lambda b, i: (b, i, 0)) for _, w, _ in groups],
        out_shape=[jax.ShapeDtypeStruct((b_, seq, w), t) for _, w, t in groups],
        compiler_params=_cparams(2),
        name="mixer_in_projection",
    )(h, mod, w_in_r)
    return {n: o for (n, _, _), o in zip(groups, outs)}


def _conv_silu(prev_ref, x_ref, next_ref, w_ref, b_ref, first, last):
    tm = x_ref.shape[0]
    rb = min(tm, CONV_ROWS)
    zero = jnp.zeros((HALO, x_ref.shape[1]), x_ref.dtype)
    xx = jnp.concatenate([jnp.where(first, zero, prev_ref[...]), x_ref[...],
                          jnp.where(last, zero, next_ref[...])], axis=0)
    win = rb + 2 * HALO
    r = lax.broadcasted_iota(jnp.int32, (CONV_K * rb, win), 0)
    col = lax.broadcasted_iota(jnp.int32, (CONV_K * rb, win), 1)
    shift = jnp.zeros_like(r)
    for k in range(1, CONV_K):
        shift = jnp.where(r >= k * rb, k * (1 - rb), shift)
    shifts = jnp.where(col == r + shift + (HALO - CONV_K // 2), 1.0, 0.0).astype(x_ref.dtype)
    out = []
    for blk in range(tm // rb):
        taps = _dot(shifts, xx[blk * rb:blk * rb + win, :])
        acc = taps[0:rb, :] * w_ref[0:1, :]
        for k in range(1, CONV_K):
            acc = acc + taps[k * rb:(k + 1) * rb, :] * w_ref[k:k + 1, :]
        out.append(acc)
    acc = out[0] if len(out) == 1 else jnp.concatenate(out, axis=0)
    if b_ref is not None:
        acc = acc + b_ref[...]
    return _silu(acc)


def _prep_body(*refs, use_rope):
    (gp_ref, gx_ref, gn_ref, sp_ref, sx_ref, sn_ref, rq_ref, rk_ref, small_ref) = refs[:9]
    k = 9
    if use_rope:
        cos_ref, sin_ref = refs[k:k + 2]
        k += 2
    gw_ref, sw_ref, sb_ref, bias_ref, alog_ref = refs[k:k + 5]
    (gq_o, gk_o, gv_o, rq_o, rk_o, sx_o, sb_o, sc_o, sm_o) = refs[k + 5:]
    i = pl.program_id(1)
    first = i == 0
    last = i == pl.num_programs(1) - 1

    qkv = _conv_silu(gp_ref, gx_ref, gn_ref, gw_ref, None, first, last)
    nq = GDN_HEADS * GDN_DK
    for hh in range(GDN_HEADS):
        sl = slice(hh * GDN_DK, (hh + 1) * GDN_DK)
        q = qkv[:, sl]
        kk = qkv[:, nq + hh * GDN_DK: nq + (hh + 1) * GDN_DK]
        q = q * lax.rsqrt(jnp.sum(q * q, axis=-1, keepdims=True) + L2_EPS) * (GDN_DK ** -0.5)
        kk = kk * lax.rsqrt(jnp.sum(kk * kk, axis=-1, keepdims=True) + L2_EPS)
        gq_o[:, sl] = q.astype(BF16)
        gk_o[:, sl] = kk.astype(BF16)
    gv_o[...] = qkv[:, 2 * nq:].astype(BF16)

    rq = rq_ref[...].astype(F32)
    rk = rk_ref[...].astype(F32) * (RET_DK ** -0.5)
    if use_rope:
        cos2 = cos_ref[...]
        sin2 = sin_ref[...]
        for hh in range(RET_HEADS):
            sl = slice(hh * RET_DK, (hh + 1) * RET_DK)
            for src, dst in ((rq, rq_o), (rk, rk_o)):
                xh = src[:, sl]
                dst[:, sl] = (xh * cos2 + pltpu.roll(xh, RET_DK // 2, axis=1) * sin2).astype(BF16)
    else:
        rq_o[...] = rq.astype(BF16)
        rk_o[...] = rk.astype(BF16)

    xbc = _conv_silu(sp_ref, sx_ref, sn_ref, sw_ref, sb_ref, first, last)
    sx_o[...] = xbc[:, :SSD_DINNER].astype(BF16)
    sb_o[...] = xbc[:, SSD_DINNER:SSD_DINNER + SSD_GROUPS * SSD_STATE].astype(BF16)
    sc_o[...] = xbc[:, SSD_DINNER + SSD_GROUPS * SSD_STATE:].astype(BF16)

    sm = small_ref[...]
    col = lax.broadcasted_iota(jnp.int32, sm.shape, 1)
    shifted = pltpu.roll(sm, COL_SLOGA - COL_DELTA, axis=1)
    xin = jnp.where(col < COL_SLOGA, sm, shifted)
    sp = _softplus(xin + bias_ref[...])
    neg_a = -jnp.exp(alog_ref[...])
    out = jnp.where(col < COL_GLOGA, jax.nn.sigmoid(sm),
                    jnp.where(col < COL_DELTA, neg_a * sp,
                              jnp.where(col < COL_SLOGA, sp,
                                        jnp.where(col < COL_END, sp * neg_a, 0.0))))
    sm_o[...] = out


def mixer_prep(proj, rope2, lp, *, tm):
    g_qkv = proj["g_qkv"]
    b_, seq, _ = g_qkv.shape
    tm = min(tm, seq)
    nh = tm // HALO
    n_halo = seq // HALO
    use_rope = rope2 is not None

    def tok(w):
        return pl.BlockSpec((None, tm, w), lambda b, i: (b, i, 0))

    def prev(w):
        return pl.BlockSpec((None, HALO, w), lambda b, i: (b, jnp.maximum(i * nh - 1, 0), 0))

    def nxt(w):
        return pl.BlockSpec((None, HALO, w), lambda b, i: (b, jnp.minimum((i + 1) * nh, n_halo - 1), 0))

    def row(w):
        return pl.BlockSpec((tm, w), lambda b, i: (i, 0))

    zeros = lambda n: jnp.zeros((n,), F32)
    bias_row = jnp.concatenate([zeros(COL_GLOGA), lp["gdn_dt_bias"].astype(F32).reshape(-1),
                                lp["ssd_dt_bias"].astype(F32).reshape(-1),
                                lp["ssd_dt_bias"].astype(F32).reshape(-1),
                                zeros(SMALL_W - COL_END)]).reshape(1, SMALL_W)
    alog_row = jnp.concatenate([zeros(COL_GLOGA), lp["gdn_a_log"].astype(F32).reshape(-1),
                                zeros(COL_SLOGA - COL_DELTA), lp["ssd_a_log"].astype(F32).reshape(-1),
                                zeros(SMALL_W - COL_END)]).reshape(1, SMALL_W)
    args = [g_qkv, g_qkv, g_qkv, proj["s_xbc"], proj["s_xbc"], proj["s_xbc"],
            proj["r_q"], proj["r_k"], proj["small"]]
    in_specs = [prev(GDN_QKV), tok(GDN_QKV), nxt(GDN_QKV), prev(SSD_XBC), tok(SSD_XBC), nxt(SSD_XBC),
                tok(RET_HEADS * RET_DK), tok(RET_HEADS * RET_DK), tok(SMALL_W)]
    if use_rope:
        args += [rope2[0], rope2[1]]
        in_specs += [row(RET_DK), row(RET_DK)]
    consts = [lp["gdn_conv_w"].astype(F32), lp["ssd_conv_w"].astype(F32),
              lp["ssd_conv_b"].astype(F32).reshape(1, SSD_XBC), bias_row, alog_row]
    args += consts
    in_specs += [_resident(a.shape) for a in consts]
    names = ("gq", "gk", "gv", "rq", "rk", "sx", "sb", "sc", "smallp")
    widths = (GDN_HEADS * GDN_DK, GDN_HEADS * GDN_DK, GDN_HEADS * GDN_DV, RET_HEADS * RET_DK,
              RET_HEADS * RET_DK, SSD_DINNER, SSD_GROUPS * SSD_STATE, SSD_GROUPS * SSD_STATE, SMALL_W)
    dtypes = (BF16,) * 8 + (F32,)
    outs = pl.pallas_call(
        functools.partial(_prep_body, use_rope=use_rope),
        grid=(b_, seq // tm),
        in_specs=in_specs,
        out_specs=[tok(w) for w in widths],
        out_shape=[jax.ShapeDtypeStruct((b_, seq, w), t) for w, t in zip(widths, dtypes)],
        compiler_params=_cparams(2),
        name="mixer_prep",
    )(*args)
    return dict(zip(names, outs))


def rope_tables(seq_len):
    rows = seq_len // GRID_W
    row_id = jnp.repeat(jnp.arange(rows, dtype=F32), GRID_W)
    col_id = jnp.tile(jnp.arange(GRID_W, dtype=F32), rows)
    n_freq = RET_DK // 4
    inv_freq = ROPE_BASE ** (-jnp.arange(n_freq, dtype=F32) / n_freq)
    ang = jnp.concatenate([row_id[:, None] * inv_freq, col_id[:, None] * inv_freq], axis=-1)
    cos, sin = jnp.cos(ang), jnp.sin(ang)
    return jnp.concatenate([cos, cos], axis=-1), jnp.concatenate([-sin, sin], axis=-1)


def _tri_masks(c, reverse):
    r = lax.broadcasted_iota(jnp.int32, (c, c), 0)
    col = lax.broadcasted_iota(jnp.int32, (c, c), 1)
    if reverse:
        return col > r, col >= r
    return col < r, col <= r


def _split3(x):
    hi = x.astype(BF16)
    r1 = x - hi.astype(F32)
    mid = r1.astype(BF16)
    lo = (r1 - mid.astype(F32)).astype(BF16)
    return hi, mid, lo


def _chunk_sums(x, c, reverse):
    t = x.shape[0]
    shift = int(np.log2(c))
    r = lax.broadcasted_iota(jnp.int32, (t, t), 0)
    col = lax.broadcasted_iota(jnp.int32, (t, t), 1)
    same = lax.shift_right_logical(r, shift) == lax.shift_right_logical(col, shift)
    tri = (col >= r) if reverse else (col <= r)
    m_tot = jnp.where(same, 1.0, 0.0).astype(BF16)
    m_cum = jnp.where(same, jnp.where(tri, 1.0, 0.0), 0.0).astype(BF16)
    parts = _split3(x)
    cum = _dot(m_cum, parts[0]) + _dot(m_cum, parts[1]) + _dot(m_cum, parts[2])
    tot = _dot(m_tot, parts[0]) + _dot(m_tot, parts[1]) + _dot(m_tot, parts[2])
    return cum, tot


def _decay_matrix(gcol, grow, incl):
    return jnp.where(incl, jnp.exp(jnp.where(incl, gcol - grow, 0.0)), 0.0)


def _merge_level_masks(c):
    r = lax.broadcasted_iota(jnp.int32, (c, c), 0)
    col = lax.broadcasted_iota(jnp.int32, (c, c), 1)
    x = lax.bitwise_xor(r, col)
    return [lax.shift_right_logical(x, k) == 1 for k in range(int(np.log2(c)))]


def _unit_triangular_inverses_minus_eye(a_mats, levels):
    shape = a_mats[0].shape
    eye = jnp.where(lax.broadcasted_iota(jnp.int32, shape, 0) == lax.broadcasted_iota(jnp.int32, shape, 1),
                    1.0, 0.0)
    ms = [-jnp.where(levels[0], a, 0.0) for a in a_mats]
    for lvl in levels[1:]:
        tbs = [(m + eye).astype(BF16) for m in ms]
        inner = [_dot(jnp.where(lvl, a, 0.0).astype(BF16), tb).astype(BF16) for a, tb in zip(a_mats, tbs)]
        ms = [m - _dot(tb, x) for m, tb, x in zip(ms, tbs, inner)]
    return ms


def _chunk_order(n, reverse):
    return range(n - 1, -1, -1) if reverse else range(n)


def _gdn_chunk_problems(d, q_ref, k_ref, v_ref, p_ref, *, c, with_output):
    reverse = d == 1
    t = k_ref.shape[0]
    sp = p_ref[...]
    gc, tot = _chunk_sums(sp, c, reverse)
    strict, incl = _tri_masks(c, reverse)
    problems = []
    for ci in _chunk_order(t // c, reverse):
        rows = slice(ci * c, (ci + 1) * c)
        gcc = gc[rows, :]
        gct = gcc.T
        totc = tot[rows, :]
        for hh in range(GDN_HEADS):
            cb = COL_BETA + d * GDN_HEADS + hh
            cl = COL_GLOGA + d * GDN_HEADS + hh
            hs = slice(hh * GDN_DK, (hh + 1) * GDN_DK)
            kb = k_ref[rows, hs]
            kf = kb.astype(F32)
            vf = v_ref[rows, hs].astype(F32)
            gcol = gcc[:, cl:cl + 1]
            tcol = totc[:, cl:cl + 1]
            beta = sp[rows, cb:cb + 1]
            e_incl = _decay_matrix(gcol, gct[cl:cl + 1, :], incl)
            pr = {"d": d, "hh": hh, "rows": rows, "hs": hs,
                  "a": _dot_nt(kb, kb) * jnp.where(strict, e_incl, 0.0) * beta,
                  "rhs": jnp.concatenate([beta * vf, (beta * jnp.exp(gcol)) * kf], axis=1),
                  "k_w": (kf * jnp.exp(tcol - gcol)).astype(BF16),
                  "c_dec": jnp.exp(tcol[0:1, :])}
            if with_output:
                qb = q_ref[rows, hs]
                pr["q_w"] = (qb.astype(F32) * jnp.exp(gcol)).astype(BF16)
                pr["p"] = (_dot_nt(qb, kb) * e_incl).astype(BF16)
            problems.append(pr)
    return problems


def _gdn_body(*refs, c, with_output):
    s0_ref = refs[0]
    ins = refs[1:9]
    if with_output:
        outs = refs[9:11]
        st_ref = refs[11]
    else:
        outs = (None, None)
        st_ref = refs[9]

    @pl.when(pl.program_id(1) == 0)
    def _():
        st_ref[...] = s0_ref[...]

    per_dir = [_gdn_chunk_problems(d, *ins[4 * d:4 * d + 4], c=c, with_output=with_output) for d in range(2)]
    n_steps = len(per_dir[0]) // GDN_HEADS
    problems = per_dir[0] + per_dir[1]
    minv = _unit_triangular_inverses_minus_eye([pr["a"] for pr in problems], _merge_level_masks(c))
    for pr, m in zip(problems, minv):
        x = pr["rhs"] + _dot(m.astype(BF16), pr["rhs"].astype(BF16))
        pr["w_v"] = x[:, :GDN_DV]
        pr["w_k"] = x[:, GDN_DV:].astype(BF16)
    state = {(d, hh): st_ref[d, hh] for d in range(2) for hh in range(GDN_HEADS)}
    for step in range(n_steps):
        now = [pr for prs in per_dir for pr in prs[step * GDN_HEADS:(step + 1) * GDN_HEADS]]
        sbs = [state[pr["d"], pr["hh"]].astype(BF16) for pr in now]
        if with_output:
            wss = [_dot(jnp.concatenate([pr["w_k"], pr["q_w"]], axis=0), sb) for pr, sb in zip(now, sbs)]
            us = [pr["w_v"] - ws[:c] for pr, ws in zip(now, wss)]
        else:
            us = [pr["w_v"] - _dot(pr["w_k"], sb) for pr, sb in zip(now, sbs)]
        ubs = [u.astype(BF16) for u in us]
        if with_output:
            for pr, ws, ub in zip(now, wss, ubs):
                outs[pr["d"]][pr["rows"], pr["hs"]] = ws[c:] + _dot(pr["p"], ub)
        for pr, ub in zip(now, ubs):
            key = (pr["d"], pr["hh"])
            state[key] = pr["c_dec"] * state[key] + _dot_tn(pr["k_w"], ub)
    for (d, hh), s in state.items():
        st_ref[d, hh] = s


def _ret_direction(d, q_ref, k_ref, v_ref, lg_ref, o_ref, st_ref, *, c, with_output):
    reverse = d == 1
    t = q_ref.shape[0]
    r = lax.broadcasted_iota(jnp.int32, (c, c), 0)
    col = lax.broadcasted_iota(jnp.int32, (c, c), 1)
    dist = ((col - r) if reverse else (r - col)).astype(F32)
    incl = dist >= 0.0
    pos = lax.broadcasted_iota(jnp.int32, (c, 1), 0).astype(F32)
    steps = (float(c) - pos) if reverse else (pos + 1.0)
    problems, c_decs, q_dec_of = [], [], []
    for hh in range(RET_HEADS):
        hs = slice(hh * RET_DK, (hh + 1) * RET_DK)
        lg = lg_ref[d, hh]
        v_dec = jnp.exp((float(c) - steps) * lg)
        c_decs.append(jnp.exp(jnp.full((1, 1), float(c), F32) * lg))
        if with_output:
            dmat = jnp.where(incl, jnp.exp(jnp.where(incl, dist * lg, 0.0)), 0.0)
            q_dec_of.append(jnp.exp(steps * lg))
        for ci in _chunk_order(t // c, reverse):
            rows = slice(ci * c, (ci + 1) * c)
            kb = k_ref[rows, hs]
            vb = v_ref[rows, hs]
            pr = {"rows": rows, "hs": hs, "inc": _dot_tn(kb, (vb.astype(F32) * v_dec).astype(BF16))}
            if with_output:
                qb = q_ref[rows, hs]
                pr["qb"] = qb
                pr["intra"] = _dot((_dot_nt(qb, kb) * dmat).astype(BF16), vb)
            problems.append(pr)
    n_steps = t // c
    for hh in range(RET_HEADS):
        s = st_ref[d, hh]
        for pr in problems[hh * n_steps:(hh + 1) * n_steps]:
            if with_output:
                o_ref[pr["rows"], pr["hs"]] = pr["intra"] + _dot(pr["qb"], s.astype(BF16)) * q_dec_of[hh]
            s = c_decs[hh] * s + pr["inc"]
        st_ref[d, hh] = s


def _ret_body(*refs, c, with_output):
    lg_ref, s0_ref = refs[0], refs[1]
    ins = refs[2:8]
    if with_output:
        outs = refs[8:10]
        st_ref = refs[10]
    else:
        outs = (None, None)
        st_ref = refs[8]

    @pl.when(pl.program_id(1) == 0)
    def _():
        st_ref[...] = s0_ref[...]

    for d in range(2):
        q_ref, k_ref, v_ref = ins[3 * d:3 * d + 3]
        _ret_direction(d, q_ref, k_ref, v_ref, lg_ref, outs[d], st_ref, c=c, with_output=with_output)


def _ssd_chunk_problems(d, sel_ref, x_ref, b_ref, c_ref, p_ref, o_ref, *, c, with_output):
    reverse = d == 1
    t = x_ref.shape[0]
    sp = p_ref[...]
    gc, tot = _chunk_sums(sp, c, reverse)
    col = lax.broadcasted_iota(jnp.int32, sp.shape, 1)
    lo = COL_SLOGA + d * SSD_HEADS
    mine = jnp.where(col >= lo, jnp.where(col < lo + SSD_HEADS, 1.0, 0.0), 0.0)
    gcm = gc * mine
    totm = tot * mine
    sel_d = sel_ref[d, 0]
    sel_l = sel_ref[d, 1]
    xd = x_ref[...].astype(F32) * _dot(sp.astype(BF16), sel_d)
    xdb = xd.astype(BF16)
    vw = (xd * _dot(jnp.exp(totm - gcm).astype(BF16), sel_l)).astype(BF16)
    if with_output:
        e1x = _dot(jnp.exp(gcm).astype(BF16), sel_l)
        _, incl = _tri_masks(c, reverse)
        lane = lax.broadcasted_iota(jnp.int32, (c, 2 * SSD_HEADDIM), 1)
        first_head = lane < SSD_HEADDIM
    gw = SSD_HG * SSD_HEADDIM
    problems = []
    for ci in _chunk_order(t // c, reverse):
        rows = slice(ci * c, (ci + 1) * c)
        gcc = gc[rows, :]
        gct = gcc.T
        ctot = jnp.exp(totm[ci * c:ci * c + 8, :])
        c_dec = sum(_dot(part, sel_l) for part in _split3(ctot))[0:1, :]
        for gg in range(SSD_GROUPS):
            gs = slice(gg * SSD_STATE, (gg + 1) * SSD_STATE)
            gcols = slice(gg * gw, (gg + 1) * gw)
            bb = b_ref[rows, gs]
            pr = {"d": d, "gg": gg, "rows": rows, "gcols": gcols,
                  "inc": _dot_tn(bb, vw[rows, gcols]), "c_dec": c_dec[:, gcols]}
            if with_output:
                cb = c_ref[rows, gs]
                scores = _dot_nt(cb, bb)
                for pair in range(SSD_HG // 2):
                    h0 = gg * SSD_HG + 2 * pair
                    cols = slice(h0 * SSD_HEADDIM, (h0 + 2) * SSD_HEADDIM)
                    ms = []
                    for hd in (h0, h0 + 1):
                        cl = lo + hd
                        dmat = _decay_matrix(gcc[:, cl:cl + 1], gct[cl:cl + 1, :], incl)
                        ms.append((scores * dmat).astype(BF16))
                    xp = xdb[rows, cols]
                    zero = jnp.zeros_like(xp)
                    rhs = jnp.concatenate([jnp.where(first_head, xp, zero), jnp.where(first_head, zero, xp)], axis=0)
                    o_ref[rows, cols] = _dot(jnp.concatenate(ms, axis=1), rhs)
                pr["cb"] = cb
                pr["q_dec"] = e1x[rows, gcols]
            problems.append(pr)
    return problems


def _ssd_body(*refs, c, with_output):
    sel_ref, s0_ref = refs[0], refs[1]
    ins = refs[2:10]
    if with_output:
        outs = refs[10:12]
        st_ref = refs[12]
    else:
        outs = (None, None)
        st_ref = refs[10]

    @pl.when(pl.program_id(1) == 0)
    def _():
        st_ref[...] = s0_ref[...]

    per_dir = [_ssd_chunk_problems(d, sel_ref, *ins[4 * d:4 * d + 4], outs[d], c=c, with_output=with_output)
               for d in range(2)]
    n_steps = len(per_dir[0]) // SSD_GROUPS
    state = {(d, gg): st_ref[d, gg] for d in range(2) for gg in range(SSD_GROUPS)}
    for step in range(n_steps):
        for prs in per_dir:
            for pr in prs[step * SSD_GROUPS:(step + 1) * SSD_GROUPS]:
                key = (pr["d"], pr["gg"])
                if with_output:
                    o_ref = outs[pr["d"]]
                    inter = _dot(pr["cb"], state[key].astype(BF16))
                    o_ref[pr["rows"], pr["gcols"]] = o_ref[pr["rows"], pr["gcols"]] + inter * pr["q_dec"]
                state[key] = pr["c_dec"] * state[key] + pr["inc"]
    for (d, gg), s in state.items():
        st_ref[d, gg] = s


def ssd_head_selectors():
    sel = np.zeros((2, 2, SMALL_W, SSD_DINNER), np.float32)
    heads = np.arange(SSD_DINNER) // SSD_HEADDIM
    for d in range(2):
        sel[d, 0, COL_DELTA + d * SSD_HEADS + heads, np.arange(SSD_DINNER)] = 1.0
        sel[d, 1, COL_SLOGA + d * SSD_HEADS + heads, np.arange(SSD_DINNER)] = 1.0
    return jnp.asarray(sel, BF16)


def _scan_call(body, name, s0, arrays, out_width, with_output, extra_args=(), extra_specs=()):
    b_, seq, _ = arrays[0].shape
    t = min(SCAN_BLOCK, seq)
    c = min(SCAN_CHUNK, t)
    nt = seq // t
    fwd = lambda w: pl.BlockSpec((None, t, w), lambda b, i: (b, i, 0))
    bwd = lambda w: pl.BlockSpec((None, t, w), lambda b, i: (b, nt - 1 - i, 0))
    st_spec = pl.BlockSpec((None,) + s0.shape[1:], lambda b, i: (b,) + (0,) * (s0.ndim - 1))
    in_specs = list(extra_specs) + [st_spec]
    in_specs += [fwd(a.shape[-1]) for a in arrays] + [bwd(a.shape[-1]) for a in arrays]
    out_specs, out_shape = [], []
    if with_output:
        out_specs += [fwd(out_width), bwd(out_width)]
        out_shape += [jax.ShapeDtypeStruct((b_, seq, out_width), F32)] * 2
    out_specs.append(st_spec)
    out_shape.append(jax.ShapeDtypeStruct(s0.shape, F32))
    res = pl.pallas_call(
        functools.partial(body, c=c, with_output=with_output),
        grid=(b_, nt),
        in_specs=in_specs,
        out_specs=out_specs,
        out_shape=out_shape,
        compiler_params=_cparams(2),
        name=name,
    )(*extra_args, s0, *arrays, *arrays)
    if with_output:
        return res[0], res[1], res[2]
    return None, None, res[0]


def token_mixers(proj, prep, states, lp, with_output):
    gdn_s0, ret_s0, ssd_s0 = states
    a_f, a_b, gdn_s = _scan_call(_gdn_body, "gdn_scan", gdn_s0,
                                 [prep["gq"], prep["gk"], prep["gv"], prep["smallp"]],
                                 GDN_HEADS * GDN_DV, with_output)
    ret_lg = -jnp.exp(lp["ret_decay"].astype(F32))
    sel = ssd_head_selectors()
    b_f, b_b, ret_s = _scan_call(_ret_body, "ret_scan", ret_s0,
                                 [prep["rq"], prep["rk"], proj["r_v"]],
                                 RET_HEADS * RET_DV, with_output, extra_args=(ret_lg,),
                                 extra_specs=(pl.BlockSpec(memory_space=pltpu.SMEM),))
    c_f, c_b, ssd_s = _scan_call(_ssd_body, "ssd_scan", ssd_s0,
                                 [prep["sx"], prep["sb"], prep["sc"], prep["smallp"]],
                                 SSD_DINNER, with_output, extra_args=(sel,),
                                 extra_specs=(_resident(sel.shape),))
    o = None
    if with_output:
        o = {"a_f": a_f, "a_b": a_b, "b_f": b_f, "b_b": b_b, "c_f": c_f, "c_b": c_b}
    return o, (gdn_s, ret_s, ssd_s)


def zero_states(b_):
    return (jnp.zeros((b_, 2, GDN_HEADS, GDN_DK, GDN_DV), F32),
            jnp.zeros((b_, 2, RET_HEADS, RET_DK, RET_DV), F32),
            jnp.zeros((b_, 2, SSD_GROUPS, SSD_STATE, SSD_HG * SSD_HEADDIM), F32))


def _merge_body(h_ref, mod_ref, oaf_ref, oab_ref, obf_ref, obb_ref, ocf_ref, ocb_ref, sx_ref,
                gz_ref, rg_ref, sz_ref, brg_ref, gng_ref, rng_ref, sd_ref, sng_ref,
                wa_ref, wb_ref, wc_ref, wo_ref, lng_ref, lnb_ref, o_ref, *, alpha):
    d = h_ref.shape[-1]
    oa = oaf_ref[...] + oab_ref[...]
    gz = gz_ref[...].astype(F32)
    ya = []
    for hh in range(GDN_HEADS):
        x = oa[:, hh * GDN_DV:(hh + 1) * GDN_DV]
        ms = jnp.mean(x * x, axis=-1, keepdims=True)
        ya.append(x * lax.rsqrt(ms + RMS_EPS) * gng_ref[...] * _silu(gz[:, hh * GDN_DV:(hh + 1) * GDN_DV]))
    ya = jnp.concatenate(ya, axis=-1).astype(BF16)
    ob = obf_ref[...] + obb_ref[...]
    rg = rg_ref[...].astype(F32)
    yb = []
    for hh in range(RET_HEADS):
        sl = slice(hh * RET_DV, (hh + 1) * RET_DV)
        x = ob[:, sl]
        mu = jnp.mean(x, axis=-1, keepdims=True)
        xc = x - mu
        var = jnp.mean(xc * xc, axis=-1, keepdims=True)
        yb.append(xc * lax.rsqrt(var + LN_EPS) * rng_ref[:, sl] * _silu(rg[:, sl]))
    yb = jnp.concatenate(yb, axis=-1).astype(BF16)
    oc = (ocf_ref[...] + ocb_ref[...] + sd_ref[...] * sx_ref[...].astype(F32)) * _silu(sz_ref[...].astype(F32))
    gw = SSD_DINNER // SSD_GROUPS
    yc = []
    for gg in range(SSD_GROUPS):
        sl = slice(gg * gw, (gg + 1) * gw)
        x = oc[:, sl]
        ms = jnp.mean(x * x, axis=-1, keepdims=True)
        yc.append(x * lax.rsqrt(ms + RMS_EPS) * sng_ref[:, sl])
    yc = jnp.concatenate(yc, axis=-1).astype(BF16)
    gates = jax.nn.sigmoid(brg_ref[...].astype(F32))
    merged = (gates[:, 0:d] * _dot(ya, wa_ref[...])
              + gates[:, d:2 * d] * _dot(yb, wb_ref[...])
              + gates[:, 2 * d:3 * d] * _dot(yc, wc_ref[...]))
    mix = _dot(merged.astype(BF16), wo_ref[...])
    h = h_ref[...]
    y = alpha * h + mod_ref[5:6, :] * mix
    o_ref[...] = _layer_norm_rows(y, lng_ref[...], lnb_ref[...])


def mixer_merge(h, mod, o, proj, sx, lp, ln_g, ln_b, *, alpha, tm):
    b_, seq, d = h.shape
    tm = min(tm, seq)

    def tok(w):
        return pl.BlockSpec((None, tm, w), lambda b, i: (b, i, 0))

    ssd_d_cols = jnp.repeat(lp["ssd_d"].astype(F32), SSD_HEADDIM).reshape(1, SSD_DINNER)
    args = [h, mod, o["a_f"], o["a_b"], o["b_f"], o["b_b"], o["c_f"], o["c_b"], sx,
            proj["g_z"], proj["r_g"], proj["s_z"], proj["br_gate"],
            lp["gdn_norm_g"].astype(F32).reshape(1, GDN_DV),
            lp["ret_norm_g"].astype(F32).reshape(1, RET_HEADS * RET_DV),
            ssd_d_cols, lp["ssd_norm_g"].astype(F32).reshape(1, SSD_DINNER),
            lp["w_br_a"], lp["w_br_b"], lp["w_br_c"], lp["w_out"], ln_g.reshape(1, d), ln_b.reshape(1, d)]
    in_specs = [tok(d), pl.BlockSpec((None, N_ADA, d), lambda b, i: (b, 0, 0)),
                tok(GDN_HEADS * GDN_DV), tok(GDN_HEADS * GDN_DV),
                tok(RET_HEADS * RET_DV), tok(RET_HEADS * RET_DV),
                tok(SSD_DINNER), tok(SSD_DINNER), tok(SSD_DINNER),
                tok(GDN_HEADS * GDN_DV), tok(RET_HEADS * RET_DV), tok(SSD_DINNER), tok(N_BRANCH * d)]
    in_specs += [_resident(a.shape) for a in args[13:]]
    return pl.pallas_call(
        functools.partial(_merge_body, alpha=alpha),
        grid=(b_, seq // tm),
        in_specs=in_specs,
        out_specs=tok(d),
        out_shape=jax.ShapeDtypeStruct(h.shape, F32),
        compiler_params=_cparams(2),
        name="mixer_merge",
    )(*args)


def kernel(x, c, ctx, c_ctx, ada_w, ada_b, ln_g, ln_b, ffn_w13, ffn_w2, mix_w_in,
           gdn_conv_w, gdn_a_log, gdn_dt_bias, gdn_norm_g, ret_decay, ret_norm_g,
           ssd_conv_w, ssd_conv_b, ssd_a_log, ssd_dt_bias, ssd_d, ssd_norm_g,
           w_br_a, w_br_b, w_br_c, mix_w_out):
    depth = ada_w.shape[0]
    alpha = float((2 * depth) ** 0.25)
    b_, seq, d = x.shape
    rope2 = rope_tables(seq)
    c_rows = jnp.concatenate([c, c_ctx[None, :]], axis=0)
    h, hc = x, ctx
    for i in range(depth):
        last = i == depth - 1
        mod_all = ada_modulation(c_rows, ada_w[i], ada_b[i].reshape(1, -1))
        mod = mod_all[:b_].reshape(b_, N_ADA, d)
        mod_c = jnp.broadcast_to(mod_all[b_:].reshape(1, N_ADA, d), (b_, N_ADA, d))
        w13 = [ffn_w13[i, j].astype(BF16) for j in range(2)]
        w2 = [ffn_w2[i, j].astype(BF16) for j in range(2)]
        w_in_r = reorder_w_in(mix_w_in[i], d)
        lp = {'gdn_conv_w': gdn_conv_w[i], 'gdn_a_log': gdn_a_log[i],
              'gdn_dt_bias': gdn_dt_bias[i], 'gdn_norm_g': gdn_norm_g[i], 'ret_decay': ret_decay[i],
              'ret_norm_g': ret_norm_g[i], 'ssd_conv_w': ssd_conv_w[i], 'ssd_conv_b': ssd_conv_b[i],
              'ssd_a_log': ssd_a_log[i], 'ssd_dt_bias': ssd_dt_bias[i], 'ssd_d': ssd_d[i],
              'ssd_norm_g': ssd_norm_g[i], 'w_br_a': w_br_a[i].astype(BF16), 'w_br_b': w_br_b[i].astype(BF16),
              'w_br_c': w_br_c[i].astype(BF16), 'w_out': mix_w_out[i].astype(BF16)}
        ffn = functools.partial(ffn_sublayer, alpha=alpha, tm=512)
        h = ffn(h, mod, w13[0], w2[0], ln_g[i, 0], ln_b[i, 0], mod_base=0)
        hc = ffn(hc, mod_c, w13[0], w2[0], ln_g[i, 0], ln_b[i, 0], mod_base=0)
        proj_c = mixer_in_projection(hc, mod_c, w_in_r, tm=256)
        prep_c = mixer_prep(proj_c, None, lp, tm=256)
        o_c, ctx_states = token_mixers(proj_c, prep_c, zero_states(b_), lp, not last)
        proj = mixer_in_projection(h, mod, w_in_r, tm=256)
        prep = mixer_prep(proj, rope2, lp, tm=256)
        o_l, _ = token_mixers(proj, prep, ctx_states, lp, True)
        h = mixer_merge(h, mod, o_l, proj, prep["sx"], lp, ln_g[i, 1], ln_b[i, 1], alpha=alpha, tm=256)
        h = ffn(h, mod, w13[1], w2[1], ln_g[i, 2], ln_b[i, 2], mod_base=6)
        if not last:
            hc = mixer_merge(hc, mod_c, o_c, proj_c, prep_c["sx"], lp, ln_g[i, 1], ln_b[i, 1],
                             alpha=alpha, tm=256)
            hc = ffn(hc, mod_c, w13[1], w2[1], ln_g[i, 2], ln_b[i, 2], mod_base=6)
    return h
```

```python
import functools

import numpy as np
import jax
import jax.numpy as jnp
from jax import lax
from jax.experimental import pallas as pl
from jax.experimental.pallas import tpu as pltpu

F32 = jnp.float32
BF16 = jnp.bfloat16

GRID_W = 64
CONV_K = 5
GDN_HEADS = 4
GDN_DK = 128
GDN_DV = 128
RET_HEADS = 4
RET_DK = 128
RET_DV = 128
SSD_HEADS = 16
SSD_HEADDIM = 64
SSD_GROUPS = 2
SSD_STATE = 128
SSD_DINNER = SSD_HEADS * SSD_HEADDIM
SSD_HG = SSD_HEADS // SSD_GROUPS
N_BRANCH = 3
N_ADA = 9
ROPE_BASE = 10000.0
GDN_QKV = GDN_HEADS * (2 * GDN_DK + GDN_DV)
SSD_XBC = SSD_DINNER + 2 * SSD_GROUPS * SSD_STATE
LN_EPS = 1e-5
RMS_EPS = 1e-6
L2_EPS = 1e-6

VMEM_LIMIT_BYTES = 56 * 1024 * 1024
LANE = 128
SMALL_W = LANE
BF16_ROWS = 16
CONV_ROWS = 128
SCAN_CHUNK = 128
SCAN_BLOCK = 256

COL_BETA = 0
COL_GLOGA = 2 * GDN_HEADS
COL_DELTA = 4 * GDN_HEADS
COL_SLOGA = COL_DELTA + 2 * SSD_HEADS
COL_END = COL_SLOGA + 2 * SSD_HEADS


def _cparams(n_axes):
    return pltpu.CompilerParams(dimension_semantics=("arbitrary",) * n_axes,
                                vmem_limit_bytes=VMEM_LIMIT_BYTES)


def _resident(shape):
    nd = len(shape)
    return pl.BlockSpec(shape, lambda *_: (0,) * nd, pipeline_mode=pl.Buffered(1))


def _layer_norm_rows(y, g, b):
    mu = jnp.mean(y, axis=-1, keepdims=True)
    yc = y - mu
    var = jnp.mean(yc * yc, axis=-1, keepdims=True)
    return yc * lax.rsqrt(var + LN_EPS) * g + b


def _silu(x):
    return x * jax.nn.sigmoid(x)


def _softplus(x):
    return jnp.maximum(x, 0.0) + jnp.log1p(jnp.exp(-jnp.abs(x)))


def _dot(a, b):
    return jnp.dot(a, b, preferred_element_type=F32)


def _dot_nt(a, b):
    return lax.dot_general(a, b, (((1,), (1,)), ((), ())), preferred_element_type=F32)


def _dot_tn(a, b):
    return lax.dot_general(a, b, (((0,), (0,)), ((), ())), preferred_element_type=F32)


def _ada_body(c_ref, w_ref, b_ref, o_ref):
    s = _silu(c_ref[...]).astype(BF16)
    o_ref[...] = _dot(s, w_ref[...].astype(BF16)) + b_ref[...]


def ada_modulation(c_rows, w, b):
    r, d = c_rows.shape
    n = w.shape[1]
    tn = 1024
    return pl.pallas_call(
        _ada_body,
        grid=(n // tn,),
        in_specs=[pl.BlockSpec((r, d), lambda j: (0, 0)),
                  pl.BlockSpec((d, tn), lambda j: (0, j)),
                  pl.BlockSpec((1, tn), lambda j: (0, j))],
        out_specs=pl.BlockSpec((r, tn), lambda j: (0, j)),
        out_shape=jax.ShapeDtypeStruct((r, n), F32),
        compiler_params=_cparams(1),
        name="ada_modulation",
    )(c_rows, w, b)


def _ffn_body(h_ref, mod_ref, w13_ref, w2_ref, lng_ref, lnb_ref, o_ref, *, alpha, ff, chunks, mod_base):
    h = h_ref[...]
    shift = mod_ref[mod_base:mod_base + 1, :]
    scale = mod_ref[mod_base + 1:mod_base + 2, :]
    gate = mod_ref[mod_base + 2:mod_base + 3, :]
    u = (h * (1.0 + scale) + shift).astype(BF16)
    acc = None
    for c0, c1 in chunks:
        a = _dot(u, w13_ref[:, c0:c1])
        b = _dot(u, w13_ref[:, ff + c0:ff + c1])
        g = (_silu(a) * b).astype(BF16)
        p = _dot(g, w2_ref[c0:c1, :])
        acc = p if acc is None else acc + p
    y = alpha * h + (0.5 * gate) * acc
    o_ref[...] = _layer_norm_rows(y, lng_ref[...], lnb_ref[...])


def _ff_chunks(ff, width=1024):
    return tuple((c0, min(c0 + width, ff)) for c0 in range(0, ff, width))


def ffn_sublayer(h, mod, w13, w2, ln_g, ln_b, *, alpha, mod_base, tm):
    b_, seq, d = h.shape
    ff = w2.shape[0]
    tm = min(tm, seq)
    body = functools.partial(_ffn_body, alpha=alpha, ff=ff, chunks=_ff_chunks(ff), mod_base=mod_base)
    return pl.pallas_call(
        body,
        grid=(b_, seq // tm),
        in_specs=[pl.BlockSpec((None, tm, d), lambda b, i: (b, i, 0)),
                  pl.BlockSpec((None, N_ADA, d), lambda b, i: (b, 0, 0)),
                  _resident(w13.shape), _resident(w2.shape),
                  _resident((1, d)), _resident((1, d))],
        out_specs=pl.BlockSpec((None, tm, d), lambda b, i: (b, i, 0)),
        out_shape=jax.ShapeDtypeStruct(h.shape, F32),
        compiler_params=_cparams(2),
        name="ffn_sublayer",
    )(h, mod, w13, w2, ln_g.reshape(1, d), ln_b.reshape(1, d))


def reorder_w_in(w_in, d):
    sizes = (GDN_QKV, GDN_HEADS * GDN_DV, 2 * GDN_HEADS, 2 * GDN_HEADS,
             RET_HEADS * RET_DK, RET_HEADS * RET_DK, RET_HEADS * RET_DV, RET_HEADS * RET_DV,
             SSD_DINNER, SSD_XBC, 2 * SSD_HEADS, N_BRANCH * d)
    offs = np.concatenate([[0], np.cumsum(sizes)])
    w_in = w_in.astype(BF16)
    cols = [w_in[:, offs[k]:offs[k + 1]] for k in range(len(sizes))]
    (g_qkv, g_z, g_b, g_a, r_q, r_k, r_v, r_g, s_z, s_xbc, s_dt, br_gate) = cols
    n_small = 4 * GDN_HEADS + 2 * SSD_HEADS
    small = jnp.concatenate([g_b, g_a, s_dt, jnp.zeros((w_in.shape[0], SMALL_W - n_small), w_in.dtype)], axis=1)
    return jnp.concatenate([g_qkv, g_z, r_q, r_k, r_v, r_g, s_z, s_xbc, br_gate, small], axis=1)


F32_ROWS = 8
W_IN_GROUPS = (("g_qkv", GDN_QKV), ("g_z", GDN_HEADS * GDN_DV), ("r_q", RET_HEADS * RET_DK),
               ("r_k", RET_HEADS * RET_DK), ("r_v", RET_HEADS * RET_DV), ("r_g", RET_HEADS * RET_DV),
               ("s_z", SSD_DINNER), ("s_xbc", SSD_XBC), ("br_gate", None), ("small", SMALL_W))


def _w_in_offsets(d):
    offs, c0 = {}, 0
    for name, width in W_IN_GROUPS:
        width = N_BRANCH * d if width is None else width
        offs[name] = (c0, c0 + width)
        c0 += width
    return offs


def _conv_silu_rows(raw, tm, w_ref, b_ref, first, last):
    h8 = F32_ROWS
    xx = jnp.concatenate([jnp.where(first, 0.0, raw[tm:tm + h8, :]), raw[0:tm, :],
                          jnp.where(last, 0.0, raw[tm + h8:tm + 2 * h8, :])], axis=0).astype(BF16)
    rb = min(tm, CONV_ROWS)
    win = rb + 2 * h8
    r = lax.broadcasted_iota(jnp.int32, (rb, CONV_K * win), 0)
    col = lax.broadcasted_iota(jnp.int32, (rb, CONV_K * win), 1)
    tap = jnp.zeros_like(col)
    for k in range(1, CONV_K):
        tap = tap + jnp.where(col >= k * win, 1, 0)
    shifts = jnp.where(col == r + tap * (win + 1) + (h8 - CONV_K // 2), 1.0, 0.0).astype(BF16)
    wb = [w_ref[k:k + 1, :].astype(BF16) for k in range(CONV_K)]
    out = []
    for blk in range(tm // rb):
        window = xx[blk * rb:blk * rb + win, :]
        out.append(_dot(shifts, jnp.concatenate([window * wk for wk in wb], axis=0)))
    acc = out[0] if len(out) == 1 else jnp.concatenate(out, axis=0)
    if b_ref is not None:
        acc = acc + b_ref[...]
    return _silu(acc)


def _inproj_prep_body(*refs, use_rope, offs):
    hp_ref, h_ref, hn_ref, mod_ref, w_ref = refs[:5]
    k = 5
    if use_rope:
        cos_ref, sin_ref = refs[k:k + 2]
        k += 2
    gw_ref, sw_ref, sb_ref, bias_ref, alog_ref = refs[k:k + 5]
    (gq_o, gk_o, gv_o, gz_o, rq_o, rk_o, rv_o, rg_o, sz_o, sx_o, sb_o, sc_o, brg_o, sm_o) = refs[k + 5:]
    i = pl.program_id(1)
    first = i == 0
    last = i == pl.num_programs(1) - 1
    tm = h_ref.shape[0]
    shift = mod_ref[3:4, :]
    scale = mod_ref[4:5, :]
    u_ext = (jnp.concatenate([h_ref[...], hp_ref[...], hn_ref[...]], axis=0) * (1.0 + scale) + shift).astype(BF16)
    u = u_ext[0:tm, :]

    def proj(lhs, name):
        c0, c1 = offs[name]
        return _dot(lhs, w_ref[:, c0:c1])

    raw_g = proj(u_ext, "g_qkv")
    raw_s = proj(u_ext, "s_xbc")

    qkv = _conv_silu_rows(raw_g, tm, gw_ref, None, first, last)
    rq = proj(u, "r_q")
    rk = proj(u, "r_k") * (RET_DK ** -0.5)
    nq = GDN_HEADS * GDN_DK
    for hh in range(GDN_HEADS):
        sl = slice(hh * GDN_DK, (hh + 1) * GDN_DK)
        q = qkv[:, sl]
        kk = qkv[:, nq + hh * GDN_DK: nq + (hh + 1) * GDN_DK]
        q = q * lax.rsqrt(jnp.sum(q * q, axis=-1, keepdims=True) + L2_EPS) * (GDN_DK ** -0.5)
        kk = kk * lax.rsqrt(jnp.sum(kk * kk, axis=-1, keepdims=True) + L2_EPS)
        gq_o[:, sl] = q.astype(BF16)
        gk_o[:, sl] = kk.astype(BF16)
    gv_o[...] = qkv[:, 2 * nq:].astype(BF16)

    xbc = _conv_silu_rows(raw_s, tm, sw_ref, sb_ref, first, last)
    sm = proj(u, "small")
    rv_o[...] = proj(u, "r_v").astype(BF16)
    gz_o[...] = _silu(proj(u, "g_z")).astype(BF16)
    sx_o[...] = xbc[:, :SSD_DINNER].astype(BF16)
    sb_o[...] = xbc[:, SSD_DINNER:SSD_DINNER + SSD_GROUPS * SSD_STATE].astype(BF16)
    sc_o[...] = xbc[:, SSD_DINNER + SSD_GROUPS * SSD_STATE:].astype(BF16)

    rg_o[...] = _silu(proj(u, "r_g")).astype(BF16)
    sz_o[...] = _silu(proj(u, "s_z")).astype(BF16)
    if use_rope:
        cos2 = cos_ref[...]
        sin2 = sin_ref[...]
        for hh in range(RET_HEADS):
            sl = slice(hh * RET_DK, (hh + 1) * RET_DK)
            for src, dst in ((rq, rq_o), (rk, rk_o)):
                xh = src[:, sl]
                dst[:, sl] = (xh * cos2 + pltpu.roll(xh, RET_DK // 2, axis=1) * sin2).astype(BF16)
    else:
        rq_o[...] = rq.astype(BF16)
        rk_o[...] = rk.astype(BF16)

    col = lax.broadcasted_iota(jnp.int32, sm.shape, 1)
    shifted = pltpu.roll(sm, COL_SLOGA - COL_DELTA, axis=1)
    xin = jnp.where(col < COL_SLOGA, sm, shifted)
    sp = _softplus(xin + bias_ref[...])
    neg_a = -jnp.exp(alog_ref[...])
    sm_o[...] = jnp.where(col < COL_GLOGA, jax.nn.sigmoid(sm),
                          jnp.where(col < COL_DELTA, neg_a * sp,
                                    jnp.where(col < COL_SLOGA, sp,
                                              jnp.where(col < COL_END, sp * neg_a, 0.0))))
    brg_o[...] = jax.nn.sigmoid(proj(u, "br_gate")).astype(BF16)


def mixer_in_projection_prep(h, mod, w_in_r, rope2, lp, *, tm):
    b_, seq, d = h.shape
    tm = min(tm, seq)
    nh = tm // F32_ROWS
    n_halo = seq // F32_ROWS
    use_rope = rope2 is not None
    offs = _w_in_offsets(d)

    def tok(w):
        return pl.BlockSpec((None, tm, w), lambda b, i: (b, i, 0))

    zeros = lambda n: jnp.zeros((n,), F32)
    bias_row = jnp.concatenate([zeros(COL_GLOGA), lp["gdn_dt_bias"].astype(F32).reshape(-1),
                                lp["ssd_dt_bias"].astype(F32).reshape(-1),
                                lp["ssd_dt_bias"].astype(F32).reshape(-1),
                                zeros(SMALL_W - COL_END)]).reshape(1, SMALL_W)
    alog_row = jnp.concatenate([zeros(COL_GLOGA), lp["gdn_a_log"].astype(F32).reshape(-1),
                                zeros(COL_SLOGA - COL_DELTA), lp["ssd_a_log"].astype(F32).reshape(-1),
                                zeros(SMALL_W - COL_END)]).reshape(1, SMALL_W)
    args = [h, h, h, mod, w_in_r]
    in_specs = [pl.BlockSpec((None, F32_ROWS, d), lambda b, i: (b, jnp.maximum(i * nh - 1, 0), 0)),
                tok(d),
                pl.BlockSpec((None, F32_ROWS, d), lambda b, i: (b, jnp.minimum((i + 1) * nh, n_halo - 1), 0)),
                pl.BlockSpec((None, N_ADA, d), lambda b, i: (b, 0, 0)),
                _resident(w_in_r.shape)]
    if use_rope:
        args += [rope2[0], rope2[1]]
        in_specs += [pl.BlockSpec((tm, RET_DK), lambda b, i: (i, 0))] * 2
    consts = [lp["gdn_conv_w"].astype(F32), lp["ssd_conv_w"].astype(F32),
              lp["ssd_conv_b"].astype(F32).reshape(1, SSD_XBC), bias_row, alog_row]
    args += consts
    in_specs += [_resident(a.shape) for a in consts]
    names = ("gq", "gk", "gv", "g_z", "rq", "rk", "r_v", "r_g", "s_z", "sx", "sb", "sc", "br_gate", "smallp")
    hw = GDN_HEADS * GDN_DK
    widths = (hw, hw, GDN_HEADS * GDN_DV, GDN_HEADS * GDN_DV, RET_HEADS * RET_DK, RET_HEADS * RET_DK,
              RET_HEADS * RET_DV, RET_HEADS * RET_DV, SSD_DINNER, SSD_DINNER, SSD_GROUPS * SSD_STATE,
              SSD_GROUPS * SSD_STATE, N_BRANCH * d, SMALL_W)
    dtypes = (BF16,) * 13 + (F32,)
    outs = pl.pallas_call(
        functools.partial(_inproj_prep_body, use_rope=use_rope, offs=offs),
        grid=(b_, seq // tm),
        in_specs=in_specs,
        out_specs=[tok(w) for w in widths],
        out_shape=[jax.ShapeDtypeStruct((b_, seq, w), t) for w, t in zip(widths, dtypes)],
        compiler_params=_cparams(2),
        name="mixer_in_projection_prep",
    )(*args)
    res = dict(zip(names, outs))
    gates = {n: res[n] for n in ("g_z", "r_v", "r_g", "s_z", "br_gate")}
    prep = {n: res[n] for n in ("gq", "gk", "gv", "rq", "rk", "sx", "sb", "sc", "smallp")}
    return gates, prep


def rope_tables(seq_len):
    rows = seq_len // GRID_W
    row_id = jnp.repeat(jnp.arange(rows, dtype=F32), GRID_W)
    col_id = jnp.tile(jnp.arange(GRID_W, dtype=F32), rows)
    n_freq = RET_DK // 4
    inv_freq = ROPE_BASE ** (-jnp.arange(n_freq, dtype=F32) / n_freq)
    ang = jnp.concatenate([row_id[:, None] * inv_freq, col_id[:, None] * inv_freq], axis=-1)
    cos, sin = jnp.cos(ang), jnp.sin(ang)
    return jnp.concatenate([cos, cos], axis=-1), jnp.concatenate([-sin, sin], axis=-1)


def _tri_masks(c, reverse):
    r = lax.broadcasted_iota(jnp.int32, (c, c), 0)
    col = lax.broadcasted_iota(jnp.int32, (c, c), 1)
    if reverse:
        return col > r, col >= r
    return col < r, col <= r


def _split3(x):
    hi = x.astype(BF16)
    r1 = x - hi.astype(F32)
    mid = r1.astype(BF16)
    lo = (r1 - mid.astype(F32)).astype(BF16)
    return hi, mid, lo


def _chunk_sums(x, c, reverse):
    t = x.shape[0]
    shift = int(np.log2(c))
    r = lax.broadcasted_iota(jnp.int32, (t, t), 0)
    col = lax.broadcasted_iota(jnp.int32, (t, t), 1)
    same = lax.shift_right_logical(r, shift) == lax.shift_right_logical(col, shift)
    tri = (col >= r) if reverse else (col <= r)
    m_tot = jnp.where(same, 1.0, 0.0).astype(BF16)
    m_cum = jnp.where(same, jnp.where(tri, 1.0, 0.0), 0.0).astype(BF16)
    parts = _split3(x)
    cum = _dot(m_cum, parts[0]) + _dot(m_cum, parts[1]) + _dot(m_cum, parts[2])
    tot = _dot(m_tot, parts[0]) + _dot(m_tot, parts[1]) + _dot(m_tot, parts[2])
    return cum, tot


def _decay_matrix(gcol, grow, incl):
    return jnp.where(incl, jnp.exp(jnp.where(incl, gcol - grow, 0.0)), 0.0)


def _merge_level_masks(c):
    r = lax.broadcasted_iota(jnp.int32, (c, c), 0)
    col = lax.broadcasted_iota(jnp.int32, (c, c), 1)
    x = lax.bitwise_xor(r, col)
    return [lax.shift_right_logical(x, k) == 1 for k in range(int(np.log2(c)))]


def _unit_triangular_inverses_minus_eye(a_mats, upper, levels):
    shape = a_mats[0].shape
    eye = jnp.where(lax.broadcasted_iota(jnp.int32, shape, 0) == lax.broadcasted_iota(jnp.int32, shape, 1),
                    1.0, 0.0)
    n = shape[0]
    ms = [-jnp.where(levels[0], a, 0.0) for a in a_mats]
    for k, lvl in enumerate(levels[1:], start=1):
        b = 1 << k
        tbs = [(m + eye).astype(BF16) for m in ms]
        es = [jnp.where(lvl, a, 0.0).astype(BF16) for a in a_mats]
        if b % BF16_ROWS:
            inner = [_dot(e, tb).astype(BF16) for e, tb in zip(es, tbs)]
            ms = [m - _dot(tb, x) for m, tb, x in zip(ms, tbs, inner)]
            continue
        blocks = [slice(j * b, (j + 1) * b) for j in range(n // b)]
        zero = jnp.zeros((b, n), BF16)
        live = [[j for j in range(n // b) if (j % 2 == 0) == rev] for rev in upper]
        rows = lambda x, js: jnp.concatenate([x[blocks[j], :] for j in js], axis=0)
        inner = [_dot(rows(e, js), tb).astype(BF16) for e, tb, js in zip(es, tbs, live)]
        spread = [jnp.concatenate([x[js.index(j) * b:(js.index(j) + 1) * b, :] if j in js else zero
                                   for j in range(n // b)], axis=0) for x, js in zip(inner, live)]
        upd = [_dot(rows(tb, js), x) for tb, x, js in zip(tbs, spread, live)]
        ms = [jnp.concatenate([m[blocks[j], :] - u[js.index(j) * b:(js.index(j) + 1) * b, :] if j in js
                               else m[blocks[j], :] for j in range(n // b)], axis=0)
              for m, u, js in zip(ms, upd, live)]
    return ms


def _chunk_order(n, reverse):
    return range(n - 1, -1, -1) if reverse else range(n)


def _gdn_chunk_problems(d, q_ref, k_ref, v_ref, p_ref, *, c, with_output):
    reverse = d == 1
    t = k_ref.shape[0]
    sp = p_ref[...]
    gc, tot = _chunk_sums(sp, c, reverse)
    strict, incl = _tri_masks(c, reverse)
    problems = []
    for ci in _chunk_order(t // c, reverse):
        rows = slice(ci * c, (ci + 1) * c)
        gcc = gc[rows, :]
        gct = gcc.T
        totc = tot[rows, :]
        for hh in range(GDN_HEADS):
            cb = COL_BETA + d * GDN_HEADS + hh
            cl = COL_GLOGA + d * GDN_HEADS + hh
            hs = slice(hh * GDN_DK, (hh + 1) * GDN_DK)
            kb = k_ref[rows, hs]
            kf = kb.astype(F32)
            vf = v_ref[rows, hs].astype(F32)
            gcol = gcc[:, cl:cl + 1]
            tcol = totc[:, cl:cl + 1]
            beta = sp[rows, cb:cb + 1]
            e_incl = _decay_matrix(gcol, gct[cl:cl + 1, :], incl)
            pr = {"d": d, "hh": hh, "rows": rows, "hs": hs,
                  "a": _dot_nt(kb, kb) * jnp.where(strict, e_incl, 0.0) * beta,
                  "rhs": jnp.concatenate([beta * vf, (beta * jnp.exp(gcol)) * kf], axis=1),
                  "k_w": (kf * jnp.exp(tcol - gcol)).astype(BF16),
                  "c_dec": jnp.exp(tcol[0:1, :])}
            if with_output:
                qb = q_ref[rows, hs]
                pr["q_w"] = (qb.astype(F32) * jnp.exp(gcol)).astype(BF16)
                pr["p"] = (_dot_nt(qb, kb) * e_incl).astype(BF16)
            problems.append(pr)
    return problems


def _gdn_body(*refs, c, with_output):
    s0_ref = refs[0]
    ins = refs[1:9]
    if with_output:
        outs = refs[9:11]
        st_ref = refs[11]
    else:
        outs = (None, None)
        st_ref = refs[9]

    @pl.when(pl.program_id(1) == 0)
    def _():
        st_ref[...] = s0_ref[...]

    per_dir = [_gdn_chunk_problems(d, *ins[4 * d:4 * d + 4], c=c, with_output=with_output) for d in range(2)]
    n_steps = len(per_dir[0]) // GDN_HEADS
    problems = per_dir[0] + per_dir[1]
    minv = _unit_triangular_inverses_minus_eye([pr["a"] for pr in problems], [pr["d"] == 1 for pr in problems],
                                               _merge_level_masks(c))
    for pr, m in zip(problems, minv):
        x = pr["rhs"] + _dot(m.astype(BF16), pr["rhs"].astype(BF16))
        pr["w_v"] = x[:, :GDN_DV]
        pr["w_k"] = x[:, GDN_DV:].astype(BF16)
    state = {(d, hh): st_ref[d, hh] for d in range(2) for hh in range(GDN_HEADS)}
    for step in range(n_steps):
        now = [pr for prs in per_dir for pr in prs[step * GDN_HEADS:(step + 1) * GDN_HEADS]]
        sbs = [state[pr["d"], pr["hh"]].astype(BF16) for pr in now]
        us = [pr["w_v"] - _dot(pr["w_k"], sb) for pr, sb in zip(now, sbs)]
        ubs = [u.astype(BF16) for u in us]
        if with_output:
            for pr, sb, ub in zip(now, sbs, ubs):
                outs[pr["d"]][pr["rows"], pr["hs"]] = _dot(jnp.concatenate([pr["q_w"], pr["p"]], axis=1),
                                                           jnp.concatenate([sb, ub], axis=0))
        for pr, ub in zip(now, ubs):
            key = (pr["d"], pr["hh"])
            state[key] = pr["c_dec"] * state[key] + _dot_tn(pr["k_w"], ub)
    for (d, hh), s in state.items():
        st_ref[d, hh] = s


def _ret_direction(d, q_ref, k_ref, v_ref, lg_ref, o_ref, st_ref, *, c, with_output):
    reverse = d == 1
    t = q_ref.shape[0]
    r = lax.broadcasted_iota(jnp.int32, (c, c), 0)
    col = lax.broadcasted_iota(jnp.int32, (c, c), 1)
    dist = ((col - r) if reverse else (r - col)).astype(F32)
    incl = dist >= 0.0
    pos = lax.broadcasted_iota(jnp.int32, (c, 1), 0).astype(F32)
    steps = (float(c) - pos) if reverse else (pos + 1.0)
    problems, c_decs, q_dec_of = [], [], []
    for hh in range(RET_HEADS):
        hs = slice(hh * RET_DK, (hh + 1) * RET_DK)
        lg = lg_ref[d, hh]
        v_dec = jnp.exp((float(c) - steps) * lg)
        c_decs.append(jnp.exp(jnp.full((1, 1), float(c), F32) * lg))
        if with_output:
            dmat = jnp.where(incl, jnp.exp(jnp.where(incl, dist * lg, 0.0)), 0.0)
            q_dec_of.append(jnp.exp(steps * lg))
        for ci in _chunk_order(t // c, reverse):
            rows = slice(ci * c, (ci + 1) * c)
            kb = k_ref[rows, hs]
            vb = v_ref[rows, hs]
            pr = {"rows": rows, "hs": hs, "inc": _dot_tn(kb, (vb.astype(F32) * v_dec).astype(BF16))}
            if with_output:
                qb = q_ref[rows, hs]
                pr["qb"] = qb
                pr["intra"] = _dot((_dot_nt(qb, kb) * dmat).astype(BF16), vb)
            problems.append(pr)
    n_steps = t // c
    for hh in range(RET_HEADS):
        s = st_ref[d, hh]
        for pr in problems[hh * n_steps:(hh + 1) * n_steps]:
            if with_output:
                o_ref[pr["rows"], pr["hs"]] = pr["intra"] + _dot(pr["qb"], s.astype(BF16)) * q_dec_of[hh]
            s = c_decs[hh] * s + pr["inc"]
        st_ref[d, hh] = s


def _ret_body(*refs, c, with_output):
    lg_ref, s0_ref = refs[0], refs[1]
    ins = refs[2:8]
    if with_output:
        outs = refs[8:10]
        st_ref = refs[10]
    else:
        outs = (None, None)
        st_ref = refs[8]

    @pl.when(pl.program_id(1) == 0)
    def _():
        st_ref[...] = s0_ref[...]

    for d in range(2):
        q_ref, k_ref, v_ref = ins[3 * d:3 * d + 3]
        _ret_direction(d, q_ref, k_ref, v_ref, lg_ref, outs[d], st_ref, c=c, with_output=with_output)


def _ssd_chunk_problems(d, sel_ref, x_ref, b_ref, c_ref, p_ref, o_ref, *, c, with_output):
    reverse = d == 1
    t = x_ref.shape[0]
    sp = p_ref[...]
    gc, tot = _chunk_sums(sp, c, reverse)
    col = lax.broadcasted_iota(jnp.int32, sp.shape, 1)
    lo = COL_SLOGA + d * SSD_HEADS
    mine = jnp.where(col >= lo, jnp.where(col < lo + SSD_HEADS, 1.0, 0.0), 0.0)
    gcm = gc * mine
    totm = tot * mine
    sel_d = sel_ref[d, 0]
    sel_l = sel_ref[d, 1]
    xd = x_ref[...].astype(F32) * _dot(sp.astype(BF16), sel_d)
    xdb = xd.astype(BF16)
    n_chunks = t // c
    stack = [jnp.exp(totm - gcm).astype(BF16)]
    if with_output:
        stack.append(jnp.exp(gcm).astype(BF16))
    tot_base = len(stack) * t
    for ci in range(n_chunks):
        stack += list(_split3(jnp.exp(totm[ci * c:ci * c + BF16_ROWS, :])))
    expanded = _dot(jnp.concatenate(stack, axis=0), sel_l)
    vw = (xd * expanded[0:t, :]).astype(BF16)
    if with_output:
        e1x = expanded[t:2 * t, :]
        _, incl = _tri_masks(c, reverse)
        lane = lax.broadcasted_iota(jnp.int32, (c, 2 * SSD_HEADDIM), 1)
        first_head = lane < SSD_HEADDIM
    gw = SSD_HG * SSD_HEADDIM
    problems = []
    for ci in _chunk_order(t // c, reverse):
        rows = slice(ci * c, (ci + 1) * c)
        gcc = gc[rows, :]
        gct = gcc.T
        base = tot_base + 3 * BF16_ROWS * ci
        c_dec = (expanded[base:base + 1, :] + expanded[base + BF16_ROWS:base + BF16_ROWS + 1, :]
                 + expanded[base + 2 * BF16_ROWS:base + 2 * BF16_ROWS + 1, :])
        for gg in range(SSD_GROUPS):
            gs = slice(gg * SSD_STATE, (gg + 1) * SSD_STATE)
            gcols = slice(gg * gw, (gg + 1) * gw)
            bb = b_ref[rows, gs]
            pr = {"d": d, "gg": gg, "rows": rows, "gcols": gcols,
                  "inc": _dot_tn(bb, vw[rows, gcols]), "c_dec": c_dec[:, gcols]}
            if with_output:
                cb = c_ref[rows, gs]
                scores = _dot_nt(cb, bb)
                for pair in range(SSD_HG // 2):
                    h0 = gg * SSD_HG + 2 * pair
                    cols = slice(h0 * SSD_HEADDIM, (h0 + 2) * SSD_HEADDIM)
                    ms = []
                    for hd in (h0, h0 + 1):
                        cl = lo + hd
                        dmat = _decay_matrix(gcc[:, cl:cl + 1], gct[cl:cl + 1, :], incl)
                        ms.append((scores * dmat).astype(BF16))
                    xp = xdb[rows, cols]
                    zero = jnp.zeros_like(xp)
                    rhs = jnp.concatenate([jnp.where(first_head, xp, zero), jnp.where(first_head, zero, xp)], axis=0)
                    o_ref[rows, cols] = _dot(jnp.concatenate(ms, axis=1), rhs)
                pr["cb"] = cb
                pr["q_dec"] = e1x[rows, gcols]
            problems.append(pr)
    return problems


def _ssd_body(*refs, c, with_output):
    sel_ref, s0_ref = refs[0], refs[1]
    ins = refs[2:10]
    if with_output:
        outs = refs[10:12]
        st_ref = refs[12]
    else:
        outs = (None, None)
        st_ref = refs[10]

    @pl.when(pl.program_id(1) == 0)
    def _():
        st_ref[...] = s0_ref[...]

    per_dir = [_ssd_chunk_problems(d, sel_ref, *ins[4 * d:4 * d + 4], outs[d], c=c, with_output=with_output)
               for d in range(2)]
    n_steps = len(per_dir[0]) // SSD_GROUPS
    state = {(d, gg): st_ref[d, gg] for d in range(2) for gg in range(SSD_GROUPS)}
    for step in range(n_steps):
        for prs in per_dir:
            for pr in prs[step * SSD_GROUPS:(step + 1) * SSD_GROUPS]:
                key = (pr["d"], pr["gg"])
                if with_output:
                    o_ref = outs[pr["d"]]
                    inter = _dot(pr["cb"], state[key].astype(BF16))
                    o_ref[pr["rows"], pr["gcols"]] = o_ref[pr["rows"], pr["gcols"]] + inter * pr["q_dec"]
                state[key] = pr["c_dec"] * state[key] + pr["inc"]
    for (d, gg), s in state.items():
        st_ref[d, gg] = s


def ssd_head_selectors():
    sel = np.zeros((2, 2, SMALL_W, SSD_DINNER), np.float32)
    heads = np.arange(SSD_DINNER) // SSD_HEADDIM
    for d in range(2):
        sel[d, 0, COL_DELTA + d * SSD_HEADS + heads, np.arange(SSD_DINNER)] = 1.0
        sel[d, 1, COL_SLOGA + d * SSD_HEADS + heads, np.arange(SSD_DINNER)] = 1.0
    return jnp.asarray(sel, BF16)


def _scan_call(body, name, s0, arrays, out_width, with_output, extra_args=(), extra_specs=()):
    b_, seq, _ = arrays[0].shape
    t = min(SCAN_BLOCK, seq)
    c = min(SCAN_CHUNK, t)
    nt = seq // t
    fwd = lambda w: pl.BlockSpec((None, t, w), lambda b, i: (b, i, 0))
    bwd = lambda w: pl.BlockSpec((None, t, w), lambda b, i: (b, nt - 1 - i, 0))
    st_spec = pl.BlockSpec((None,) + s0.shape[1:], lambda b, i: (b,) + (0,) * (s0.ndim - 1))
    in_specs = list(extra_specs) + [st_spec]
    in_specs += [fwd(a.shape[-1]) for a in arrays] + [bwd(a.shape[-1]) for a in arrays]
    out_specs, out_shape = [], []
    if with_output:
        out_specs += [fwd(out_width), bwd(out_width)]
        out_shape += [jax.ShapeDtypeStruct((b_, seq, out_width), F32)] * 2
    out_specs.append(st_spec)
    out_shape.append(jax.ShapeDtypeStruct(s0.shape, F32))
    res = pl.pallas_call(
        functools.partial(body, c=c, with_output=with_output),
        grid=(b_, nt),
        in_specs=in_specs,
        out_specs=out_specs,
        out_shape=out_shape,
        compiler_params=_cparams(2),
        name=name,
    )(*extra_args, s0, *arrays, *arrays)
    if with_output:
        return res[0], res[1], res[2]
    return None, None, res[0]


def token_mixers(proj, prep, states, lp, with_output):
    gdn_s0, ret_s0, ssd_s0 = states
    a_f, a_b, gdn_s = _scan_call(_gdn_body, "gdn_scan", gdn_s0,
                                 [prep["gq"], prep["gk"], prep["gv"], prep["smallp"]],
                                 GDN_HEADS * GDN_DV, with_output)
    ret_lg = -jnp.exp(lp["ret_decay"].astype(F32))
    sel = ssd_head_selectors()
    b_f, b_b, ret_s = _scan_call(_ret_body, "ret_scan", ret_s0,
                                 [prep["rq"], prep["rk"], proj["r_v"]],
                                 RET_HEADS * RET_DV, with_output, extra_args=(ret_lg,),
                                 extra_specs=(pl.BlockSpec(memory_space=pltpu.SMEM),))
    c_f, c_b, ssd_s = _scan_call(_ssd_body, "ssd_scan", ssd_s0,
                                 [prep["sx"], prep["sb"], prep["sc"], prep["smallp"]],
                                 SSD_DINNER, with_output, extra_args=(sel,),
                                 extra_specs=(_resident(sel.shape),))
    o = None
    if with_output:
        o = {"a_f": a_f, "a_b": a_b, "b_f": b_f, "b_b": b_b, "c_f": c_f, "c_b": c_b}
    return o, (gdn_s, ret_s, ssd_s)


def zero_states(b_):
    return (jnp.zeros((b_, 2, GDN_HEADS, GDN_DK, GDN_DV), F32),
            jnp.zeros((b_, 2, RET_HEADS, RET_DK, RET_DV), F32),
            jnp.zeros((b_, 2, SSD_GROUPS, SSD_STATE, SSD_HG * SSD_HEADDIM), F32))


def _merge_body(h_ref, mod_ref, oaf_ref, oab_ref, obf_ref, obb_ref, ocf_ref, ocb_ref, sx_ref,
                gz_ref, rg_ref, sz_ref, brg_ref, gng_ref, rng_ref, sd_ref, sng_ref,
                wa_ref, wb_ref, wc_ref, wo_ref, lng_ref, lnb_ref, o_ref, *, alpha):
    d = h_ref.shape[-1]
    oa = oaf_ref[...] + oab_ref[...]
    gz = gz_ref[...].astype(F32)
    ya = []
    for hh in range(GDN_HEADS):
        x = oa[:, hh * GDN_DV:(hh + 1) * GDN_DV]
        ms = jnp.mean(x * x, axis=-1, keepdims=True)
        ya.append(x * lax.rsqrt(ms + RMS_EPS) * gng_ref[...] * gz[:, hh * GDN_DV:(hh + 1) * GDN_DV])
    ya = jnp.concatenate(ya, axis=-1).astype(BF16)
    ob = obf_ref[...] + obb_ref[...]
    rg = rg_ref[...].astype(F32)
    yb = []
    for hh in range(RET_HEADS):
        sl = slice(hh * RET_DV, (hh + 1) * RET_DV)
        x = ob[:, sl]
        mu = jnp.mean(x, axis=-1, keepdims=True)
        xc = x - mu
        var = jnp.mean(xc * xc, axis=-1, keepdims=True)
        yb.append(xc * lax.rsqrt(var + LN_EPS) * rng_ref[:, sl] * rg[:, sl])
    yb = jnp.concatenate(yb, axis=-1).astype(BF16)
    oc = (ocf_ref[...] + ocb_ref[...] + sd_ref[...] * sx_ref[...].astype(F32)) * sz_ref[...].astype(F32)
    gw = SSD_DINNER // SSD_GROUPS
    yc = []
    for gg in range(SSD_GROUPS):
        sl = slice(gg * gw, (gg + 1) * gw)
        x = oc[:, sl]
        ms = jnp.mean(x * x, axis=-1, keepdims=True)
        yc.append(x * lax.rsqrt(ms + RMS_EPS) * sng_ref[:, sl])
    yc = jnp.concatenate(yc, axis=-1).astype(BF16)
    gates = brg_ref[...].astype(F32)
    merged = (gates[:, 0:d] * _dot(ya, wa_ref[...])
              + gates[:, d:2 * d] * _dot(yb, wb_ref[...])
              + gates[:, 2 * d:3 * d] * _dot(yc, wc_ref[...]))
    mix = _dot(merged.astype(BF16), wo_ref[...])
    h = h_ref[...]
    y = alpha * h + mod_ref[5:6, :] * mix
    o_ref[...] = _layer_norm_rows(y, lng_ref[...], lnb_ref[...])


def mixer_merge(h, mod, o, proj, sx, lp, ln_g, ln_b, *, alpha, tm):
    b_, seq, d = h.shape
    tm = min(tm, seq)

    def tok(w):
        return pl.BlockSpec((None, tm, w), lambda b, i: (b, i, 0))

    ssd_d_cols = jnp.repeat(lp["ssd_d"].astype(F32), SSD_HEADDIM).reshape(1, SSD_DINNER)
    args = [h, mod, o["a_f"], o["a_b"], o["b_f"], o["b_b"], o["c_f"], o["c_b"], sx,
            proj["g_z"], proj["r_g"], proj["s_z"], proj["br_gate"],
            lp["gdn_norm_g"].astype(F32).reshape(1, GDN_DV),
            lp["ret_norm_g"].astype(F32).reshape(1, RET_HEADS * RET_DV),
            ssd_d_cols, lp["ssd_norm_g"].astype(F32).reshape(1, SSD_DINNER),
            lp["w_br_a"], lp["w_br_b"], lp["w_br_c"], lp["w_out"], ln_g.reshape(1, d), ln_b.reshape(1, d)]
    in_specs = [tok(d), pl.BlockSpec((None, N_ADA, d), lambda b, i: (b, 0, 0)),
                tok(GDN_HEADS * GDN_DV), tok(GDN_HEADS * GDN_DV),
                tok(RET_HEADS * RET_DV), tok(RET_HEADS * RET_DV),
                tok(SSD_DINNER), tok(SSD_DINNER), tok(SSD_DINNER),
                tok(GDN_HEADS * GDN_DV), tok(RET_HEADS * RET_DV), tok(SSD_DINNER), tok(N_BRANCH * d)]
    in_specs += [_resident(a.shape) for a in args[13:]]
    return pl.pallas_call(
        functools.partial(_merge_body, alpha=alpha),
        grid=(b_, seq // tm),
        in_specs=in_specs,
        out_specs=tok(d),
        out_shape=jax.ShapeDtypeStruct(h.shape, F32),
        compiler_params=_cparams(2),
        name="mixer_merge",
    )(*args)


def kernel(x, c, ctx, c_ctx, ada_w, ada_b, ln_g, ln_b, ffn_w13, ffn_w2, mix_w_in,
           gdn_conv_w, gdn_a_log, gdn_dt_bias, gdn_norm_g, ret_decay, ret_norm_g,
           ssd_conv_w, ssd_conv_b, ssd_a_log, ssd_dt_bias, ssd_d, ssd_norm_g,
           w_br_a, w_br_b, w_br_c, mix_w_out):
    depth = ada_w.shape[0]
    alpha = float((2 * depth) ** 0.25)
    b_, seq, d = x.shape
    rope2 = rope_tables(seq)
    c_rows = jnp.concatenate([c, c_ctx[None, :]], axis=0)
    h, hc = x, ctx
    for i in range(depth):
        last = i == depth - 1
        mod_all = ada_modulation(c_rows, ada_w[i], ada_b[i].reshape(1, -1))
        mod = mod_all[:b_].reshape(b_, N_ADA, d)
        mod_c = jnp.broadcast_to(mod_all[b_:].reshape(1, N_ADA, d), (b_, N_ADA, d))
        w13 = [ffn_w13[i, j].astype(BF16) for j in range(2)]
        w2 = [ffn_w2[i, j].astype(BF16) for j in range(2)]
        w_in_r = reorder_w_in(mix_w_in[i], d)
        lp = {'gdn_conv_w': gdn_conv_w[i], 'gdn_a_log': gdn_a_log[i],
              'gdn_dt_bias': gdn_dt_bias[i], 'gdn_norm_g': gdn_norm_g[i], 'ret_decay': ret_decay[i],
              'ret_norm_g': ret_norm_g[i], 'ssd_conv_w': ssd_conv_w[i], 'ssd_conv_b': ssd_conv_b[i],
              'ssd_a_log': ssd_a_log[i], 'ssd_dt_bias': ssd_dt_bias[i], 'ssd_d': ssd_d[i],
              'ssd_norm_g': ssd_norm_g[i], 'w_br_a': w_br_a[i].astype(BF16), 'w_br_b': w_br_b[i].astype(BF16),
              'w_br_c': w_br_c[i].astype(BF16), 'w_out': mix_w_out[i].astype(BF16)}
        ffn = functools.partial(ffn_sublayer, alpha=alpha, tm=512)
        h = ffn(h, mod, w13[0], w2[0], ln_g[i, 0], ln_b[i, 0], mod_base=0)
        hc = ffn(hc, mod_c, w13[0], w2[0], ln_g[i, 0], ln_b[i, 0], mod_base=0)
        proj_c, prep_c = mixer_in_projection_prep(hc, mod_c, w_in_r, None, lp, tm=256)
        o_c, ctx_states = token_mixers(proj_c, prep_c, zero_states(b_), lp, not last)
        proj, prep = mixer_in_projection_prep(h, mod, w_in_r, rope2, lp, tm=256)
        o_l, _ = token_mixers(proj, prep, ctx_states, lp, True)
        h = mixer_merge(h, mod, o_l, proj, prep["sx"], lp, ln_g[i, 1], ln_b[i, 1], alpha=alpha, tm=256)
        h = ffn(h, mod, w13[1], w2[1], ln_g[i, 2], ln_b[i, 2], mod_base=6)
        if not last:
            hc = mixer_merge(hc, mod_c, o_c, proj_c, prep_c["sx"], lp, ln_g[i, 1], ln_b[i, 1],
                             alpha=alpha, tm=256)
            hc = ffn(hc, mod_c, w13[1], w2[1], ln_g[i, 2], ln_b[i, 2], mod_base=6)
    return h
```

```python
import functools

import numpy as np
import jax
import jax.numpy as jnp
from jax import lax
from jax.experimental import pallas as pl
from jax.experimental.pallas import tpu as pltpu

F32 = jnp.float32
BF16 = jnp.bfloat16

GRID_W = 64
CONV_K = 5
GDN_HEADS = 4
GDN_DK = 128
GDN_DV = 128
RET_HEADS = 4
RET_DK = 128
RET_DV = 128
SSD_HEADS = 16
SSD_HEADDIM = 64
SSD_GROUPS = 2
SSD_STATE = 128
SSD_DINNER = SSD_HEADS * SSD_HEADDIM
SSD_HG = SSD_HEADS // SSD_GROUPS
N_BRANCH = 3
N_ADA = 9
ROPE_BASE = 10000.0
GDN_QKV = GDN_HEADS * (2 * GDN_DK + GDN_DV)
SSD_XBC = SSD_DINNER + 2 * SSD_GROUPS * SSD_STATE
LN_EPS = 1e-5
RMS_EPS = 1e-6
L2_EPS = 1e-6

VMEM_LIMIT_BYTES = 56 * 1024 * 1024
LANE = 128
SMALL_W = LANE
BF16_ROWS = 16
CONV_ROWS = 128
FFN_ROWS = 256
SCAN_CHUNK = 128
SCAN_BLOCK = 256
RET_SCAN_BLOCK = 512

COL_BETA = 0
COL_GLOGA = 2 * GDN_HEADS
COL_DELTA = 4 * GDN_HEADS
COL_SLOGA = COL_DELTA + 2 * SSD_HEADS
COL_END = COL_SLOGA + 2 * SSD_HEADS


def _cparams(n_axes):
    return pltpu.CompilerParams(dimension_semantics=("arbitrary",) * n_axes,
                                vmem_limit_bytes=VMEM_LIMIT_BYTES)


def _resident(shape):
    nd = len(shape)
    return pl.BlockSpec(shape, lambda *_: (0,) * nd, pipeline_mode=pl.Buffered(1))


def _resident_slice(arr, lead):
    tail = tuple(arr.shape[len(lead):])
    return pl.BlockSpec((None,) * len(lead) + tail, lambda *_: tuple(lead) + (0,) * len(tail),
                        pipeline_mode=pl.Buffered(1))


def _layer_norm_rows(y, g, b):
    mu = jnp.mean(y, axis=-1, keepdims=True)
    yc = y - mu
    var = jnp.mean(yc * yc, axis=-1, keepdims=True)
    return yc * lax.rsqrt(var + LN_EPS) * g + b


def _silu(x):
    return x * jax.nn.sigmoid(x)


def _softplus(x):
    return jnp.maximum(x, 0.0) + jnp.log1p(jnp.exp(-jnp.abs(x)))


def _dot(a, b):
    return jnp.dot(a, b, preferred_element_type=F32)


def _dot_nt(a, b):
    return lax.dot_general(a, b, (((1,), (1,)), ((), ())), preferred_element_type=F32)


def _dot_tn(a, b):
    return lax.dot_general(a, b, (((0,), (0,)), ((), ())), preferred_element_type=F32)


def _ada_body(c_ref, w_ref, b_ref, o_ref):
    s = _silu(c_ref[...]).astype(BF16)
    o_ref[...] = _dot(s, w_ref[...].astype(BF16)) + b_ref[...]


def ada_modulation(c_rows, w, b, layer):
    r, d = c_rows.shape
    n = w.shape[-1]
    tn = 1024
    return pl.pallas_call(
        _ada_body,
        grid=(n // tn,),
        in_specs=[pl.BlockSpec((r, d), lambda j: (0, 0)),
                  pl.BlockSpec((None, d, tn), lambda j: (layer, 0, j)),
                  pl.BlockSpec((None, 1, tn), lambda j: (layer, 0, j))],
        out_specs=pl.BlockSpec((r, tn), lambda j: (0, j)),
        out_shape=jax.ShapeDtypeStruct((r, n), F32),
        compiler_params=_cparams(1),
        name="ada_modulation",
    )(c_rows, w, b)


def _ffn_body(h_ref, mod_ref, w13_ref, w2_ref, lng_ref, lnb_ref, o_ref, *, alpha, ff, chunks, mod_base):
    shift = mod_ref[mod_base:mod_base + 1, :]
    scale = mod_ref[mod_base + 1:mod_base + 2, :]
    gate = mod_ref[mod_base + 2:mod_base + 3, :]
    tm = h_ref.shape[0]
    rb = min(tm, FFN_ROWS)
    for r0 in range(0, tm, rb):
        h = h_ref[r0:r0 + rb, :]
        u = (h * (1.0 + scale) + shift).astype(BF16)
        acc = None
        for c0, c1 in chunks:
            a = _dot(u, w13_ref[:, c0:c1])
            b = _dot(u, w13_ref[:, ff + c0:ff + c1])
            g = (_silu(a) * b).astype(BF16)
            p = _dot(g, w2_ref[c0:c1, :])
            acc = p if acc is None else acc + p
        y = alpha * h + (0.5 * gate) * acc
        o_ref[r0:r0 + rb, :] = _layer_norm_rows(y, lng_ref[...], lnb_ref[...])


def _ff_chunks(ff, width=1024):
    return tuple((c0, min(c0 + width, ff)) for c0 in range(0, ff, width))


def ffn_sublayer(h, mod, w13, w2, which, ln_g, ln_b, *, alpha, mod_base, tm):
    b_, seq, d = h.shape
    ff = w2.shape[-2]
    tm = min(tm, seq)
    body = functools.partial(_ffn_body, alpha=alpha, ff=ff, chunks=_ff_chunks(ff), mod_base=mod_base)
    return pl.pallas_call(
        body,
        grid=(b_, seq // tm),
        in_specs=[pl.BlockSpec((None, tm, d), lambda b, i: (b, i, 0)),
                  pl.BlockSpec((None, N_ADA, d), lambda b, i: (b, 0, 0)),
                  _resident_slice(w13, which), _resident_slice(w2, which),
                  _resident((1, d)), _resident((1, d))],
        out_specs=pl.BlockSpec((None, tm, d), lambda b, i: (b, i, 0)),
        out_shape=jax.ShapeDtypeStruct(h.shape, F32),
        compiler_params=_cparams(2),
        name="ffn_sublayer",
    )(h, mod, w13, w2, ln_g.reshape(1, d), ln_b.reshape(1, d))


def reorder_w_in(w_in, d):
    sizes = (GDN_QKV, GDN_HEADS * GDN_DV, 2 * GDN_HEADS, 2 * GDN_HEADS,
             RET_HEADS * RET_DK, RET_HEADS * RET_DK, RET_HEADS * RET_DV, RET_HEADS * RET_DV,
             SSD_DINNER, SSD_XBC, 2 * SSD_HEADS, N_BRANCH * d)
    offs = np.concatenate([[0], np.cumsum(sizes)])
    w_in = w_in.astype(BF16)
    cols = [w_in[..., offs[k]:offs[k + 1]] for k in range(len(sizes))]
    (g_qkv, g_z, g_b, g_a, r_q, r_k, r_v, r_g, s_z, s_xbc, s_dt, br_gate) = cols
    n_small = 4 * GDN_HEADS + 2 * SSD_HEADS
    pad = jnp.zeros(w_in.shape[:-1] + (SMALL_W - n_small,), w_in.dtype)
    return jnp.concatenate([g_qkv, g_z, r_q, r_k, r_v, r_g, s_z, s_xbc, br_gate, g_b, g_a, s_dt, pad], axis=-1)


F32_ROWS = 8
W_IN_GROUPS = (("g_qkv", GDN_QKV), ("g_z", GDN_HEADS * GDN_DV), ("r_q", RET_HEADS * RET_DK),
               ("r_k", RET_HEADS * RET_DK), ("r_v", RET_HEADS * RET_DV), ("r_g", RET_HEADS * RET_DV),
               ("s_z", SSD_DINNER), ("s_xbc", SSD_XBC), ("br_gate", None), ("small", SMALL_W))


def _w_in_offsets(d):
    offs, c0 = {}, 0
    for name, width in W_IN_GROUPS:
        width = N_BRANCH * d if width is None else width
        offs[name] = (c0, c0 + width)
        c0 += width
    return offs


def _conv_silu_rows(raw, tm, w_ref, b_ref, first, last):
    h8 = F32_ROWS
    xx = jnp.concatenate([jnp.where(first, 0.0, raw[tm:tm + h8, :]), raw[0:tm, :],
                          jnp.where(last, 0.0, raw[tm + h8:tm + 2 * h8, :])], axis=0).astype(BF16)
    rb = min(tm, CONV_ROWS)
    win = rb + 2 * h8
    r = lax.broadcasted_iota(jnp.int32, (rb, CONV_K * win), 0)
    col = lax.broadcasted_iota(jnp.int32, (rb, CONV_K * win), 1)
    tap = jnp.zeros_like(col)
    for k in range(1, CONV_K):
        tap = tap + jnp.where(col >= k * win, 1, 0)
    shifts = jnp.where(col == r + tap * (win + 1) + (h8 - CONV_K // 2), 1.0, 0.0).astype(BF16)
    wb = [w_ref[k:k + 1, :].astype(BF16) for k in range(CONV_K)]
    out = []
    for blk in range(tm // rb):
        window = xx[blk * rb:blk * rb + win, :]
        out.append(_dot(shifts, jnp.concatenate([window * wk for wk in wb], axis=0)))
    acc = out[0] if len(out) == 1 else jnp.concatenate(out, axis=0)
    if b_ref is not None:
        acc = acc + b_ref[...]
    return _silu(acc)


def _inproj_prep_body(*refs, use_rope, offs):
    hp_ref, h_ref, hn_ref, mod_ref, w_ref = refs[:5]
    k = 5
    if use_rope:
        cos_ref, sin_ref = refs[k:k + 2]
        k += 2
    gw_ref, sw_ref, sb_ref, bias_ref, alog_ref = refs[k:k + 5]
    (gq_o, gk_o, gv_o, gz_o, rq_o, rk_o, rv_o, rg_o, sz_o, sx_o, sb_o, sc_o, brg_o, sm_o) = refs[k + 5:]
    i = pl.program_id(1)
    first = i == 0
    last = i == pl.num_programs(1) - 1
    tm = h_ref.shape[0]
    shift = mod_ref[3:4, :]
    scale = mod_ref[4:5, :]
    u_ext = (jnp.concatenate([h_ref[...], hp_ref[...], hn_ref[...]], axis=0) * (1.0 + scale) + shift).astype(BF16)
    u = u_ext[0:tm, :]

    def proj(lhs, name):
        c0, c1 = offs[name]
        return _dot(lhs, w_ref[:, c0:c1])

    raw_g = proj(u_ext, "g_qkv")
    raw_s = proj(u_ext, "s_xbc")

    qkv = _conv_silu_rows(raw_g, tm, gw_ref, None, first, last)
    rq = proj(u, "r_q")
    rk = proj(u, "r_k") * (RET_DK ** -0.5)
    nq = GDN_HEADS * GDN_DK
    for hh in range(GDN_HEADS):
        sl = slice(hh * GDN_DK, (hh + 1) * GDN_DK)
        q = qkv[:, sl]
        kk = qkv[:, nq + hh * GDN_DK: nq + (hh + 1) * GDN_DK]
        q = q * lax.rsqrt(jnp.sum(q * q, axis=-1, keepdims=True) + L2_EPS) * (GDN_DK ** -0.5)
        kk = kk * lax.rsqrt(jnp.sum(kk * kk, axis=-1, keepdims=True) + L2_EPS)
        gq_o[:, sl] = q.astype(BF16)
        gk_o[:, sl] = kk.astype(BF16)
    gv_o[...] = qkv[:, 2 * nq:].astype(BF16)

    xbc = _conv_silu_rows(raw_s, tm, sw_ref, sb_ref, first, last)
    sm = proj(u, "small")
    rv_o[...] = proj(u, "r_v").astype(BF16)
    gz_o[...] = _silu(proj(u, "g_z")).astype(BF16)
    sx_o[...] = xbc[:, :SSD_DINNER].astype(BF16)
    sb_o[...] = xbc[:, SSD_DINNER:SSD_DINNER + SSD_GROUPS * SSD_STATE].astype(BF16)
    sc_o[...] = xbc[:, SSD_DINNER + SSD_GROUPS * SSD_STATE:].astype(BF16)

    rg_o[...] = _silu(proj(u, "r_g")).astype(BF16)
    sz_o[...] = _silu(proj(u, "s_z")).astype(BF16)
    if use_rope:
        cos2 = cos_ref[...]
        sin2 = sin_ref[...]
        for hh in range(RET_HEADS):
            sl = slice(hh * RET_DK, (hh + 1) * RET_DK)
            for src, dst in ((rq, rq_o), (rk, rk_o)):
                xh = src[:, sl]
                dst[:, sl] = (xh * cos2 + pltpu.roll(xh, RET_DK // 2, axis=1) * sin2).astype(BF16)
    else:
        rq_o[...] = rq.astype(BF16)
        rk_o[...] = rk.astype(BF16)

    col = lax.broadcasted_iota(jnp.int32, sm.shape, 1)
    shifted = pltpu.roll(sm, COL_SLOGA - COL_DELTA, axis=1)
    xin = jnp.where(col < COL_SLOGA, sm, shifted)
    sp = _softplus(xin + bias_ref[...])
    neg_a = -jnp.exp(alog_ref[...])
    sm_o[...] = jnp.where(col < COL_GLOGA, jax.nn.sigmoid(sm),
                          jnp.where(col < COL_DELTA, neg_a * sp,
                                    jnp.where(col < COL_SLOGA, sp,
                                              jnp.where(col < COL_END, sp * neg_a, 0.0))))
    brg_o[...] = jax.nn.sigmoid(proj(u, "br_gate")).astype(BF16)


def mixer_in_projection_prep(h, mod, w_in_r, rope2, lp, *, tm):
    b_, seq, d = h.shape
    tm = min(tm, seq)
    nh = tm // F32_ROWS
    n_halo = seq // F32_ROWS
    use_rope = rope2 is not None
    offs = _w_in_offsets(d)

    def tok(w):
        return pl.BlockSpec((None, tm, w), lambda b, i: (b, i, 0))

    zeros = lambda n: jnp.zeros((n,), F32)
    bias_row = jnp.concatenate([zeros(COL_GLOGA), lp["gdn_dt_bias"].astype(F32).reshape(-1),
                                lp["ssd_dt_bias"].astype(F32).reshape(-1),
                                lp["ssd_dt_bias"].astype(F32).reshape(-1),
                                zeros(SMALL_W - COL_END)]).reshape(1, SMALL_W)
    alog_row = jnp.concatenate([zeros(COL_GLOGA), lp["gdn_a_log"].astype(F32).reshape(-1),
                                zeros(COL_SLOGA - COL_DELTA), lp["ssd_a_log"].astype(F32).reshape(-1),
                                zeros(SMALL_W - COL_END)]).reshape(1, SMALL_W)
    args = [h, h, h, mod, w_in_r]
    in_specs = [pl.BlockSpec((None, F32_ROWS, d), lambda b, i: (b, jnp.maximum(i * nh - 1, 0), 0)),
                tok(d),
                pl.BlockSpec((None, F32_ROWS, d), lambda b, i: (b, jnp.minimum((i + 1) * nh, n_halo - 1), 0)),
                pl.BlockSpec((None, N_ADA, d), lambda b, i: (b, 0, 0)),
                _resident_slice(w_in_r, (lp["layer"],))]
    if use_rope:
        args += [rope2[0], rope2[1]]
        in_specs += [pl.BlockSpec((tm, RET_DK), lambda b, i: (i, 0))] * 2
    consts = [lp["gdn_conv_w"].astype(F32), lp["ssd_conv_w"].astype(F32),
              lp["ssd_conv_b"].astype(F32).reshape(1, SSD_XBC), bias_row, alog_row]
    args += consts
    in_specs += [_resident(a.shape) for a in consts]
    names = ("gq", "gk", "gv", "g_z", "rq", "rk", "r_v", "r_g", "s_z", "sx", "sb", "sc", "br_gate", "smallp")
    hw = GDN_HEADS * GDN_DK
    widths = (hw, hw, GDN_HEADS * GDN_DV, GDN_HEADS * GDN_DV, RET_HEADS * RET_DK, RET_HEADS * RET_DK,
              RET_HEADS * RET_DV, RET_HEADS * RET_DV, SSD_DINNER, SSD_DINNER, SSD_GROUPS * SSD_STATE,
              SSD_GROUPS * SSD_STATE, N_BRANCH * d, SMALL_W)
    dtypes = (BF16,) * 13 + (F32,)
    outs = pl.pallas_call(
        functools.partial(_inproj_prep_body, use_rope=use_rope, offs=offs),
        grid=(b_, seq // tm),
        in_specs=in_specs,
        out_specs=[tok(w) for w in widths],
        out_shape=[jax.ShapeDtypeStruct((b_, seq, w), t) for w, t in zip(widths, dtypes)],
        compiler_params=_cparams(2),
        name="mixer_in_projection_prep",
    )(*args)
    res = dict(zip(names, outs))
    gates = {n: res[n] for n in ("g_z", "r_v", "r_g", "s_z", "br_gate")}
    prep = {n: res[n] for n in ("gq", "gk", "gv", "rq", "rk", "sx", "sb", "sc", "smallp")}
    return gates, prep


def rope_tables(seq_len):
    rows = seq_len // GRID_W
    row_id = jnp.repeat(jnp.arange(rows, dtype=F32), GRID_W)
    col_id = jnp.tile(jnp.arange(GRID_W, dtype=F32), rows)
    n_freq = RET_DK // 4
    inv_freq = ROPE_BASE ** (-jnp.arange(n_freq, dtype=F32) / n_freq)
    ang = jnp.concatenate([row_id[:, None] * inv_freq, col_id[:, None] * inv_freq], axis=-1)
    cos, sin = jnp.cos(ang), jnp.sin(ang)
    return jnp.concatenate([cos, cos], axis=-1), jnp.concatenate([-sin, sin], axis=-1)


def _tri_masks(c, reverse):
    r = lax.broadcasted_iota(jnp.int32, (c, c), 0)
    col = lax.broadcasted_iota(jnp.int32, (c, c), 1)
    if reverse:
        return col > r, col >= r
    return col < r, col <= r


def _split3(x):
    hi = x.astype(BF16)
    r1 = x - hi.astype(F32)
    mid = r1.astype(BF16)
    lo = (r1 - mid.astype(F32)).astype(BF16)
    return hi, mid, lo


def _chunk_sums(x, c, reverse):
    t, w = x.shape
    r = lax.broadcasted_iota(jnp.int32, (c, c), 0)
    col = lax.broadcasted_iota(jnp.int32, (c, c), 1)
    m_cum = jnp.where((col >= r) if reverse else (col <= r), 1.0, 0.0).astype(BF16)
    cums, tots = [], []
    for ci in range(t // c):
        three = _dot(m_cum, jnp.concatenate(_split3(x[ci * c:(ci + 1) * c, :]), axis=1))
        cum = three[:, 0:w] + three[:, w:2 * w] + three[:, 2 * w:3 * w]
        end = 0 if reverse else c - 1
        cums.append(cum)
        tots.append(jnp.broadcast_to(cum[end:end + 1, :], (c, w)))
    if len(cums) == 1:
        return cums[0], tots[0]
    return jnp.concatenate(cums, axis=0), jnp.concatenate(tots, axis=0)


def _decay_matrix(gcol, grow, incl):
    return jnp.where(incl, jnp.exp(jnp.where(incl, gcol - grow, 0.0)), 0.0)


def _merge_level_masks(c):
    r = lax.broadcasted_iota(jnp.int32, (c, c), 0)
    col = lax.broadcasted_iota(jnp.int32, (c, c), 1)
    x = lax.bitwise_xor(r, col)
    return [lax.shift_right_logical(x, k) == 1 for k in range(int(np.log2(c)))]


def _unit_triangular_inverses_minus_eye(a_mats, upper, levels):
    shape = a_mats[0].shape
    eye = jnp.where(lax.broadcasted_iota(jnp.int32, shape, 0) == lax.broadcasted_iota(jnp.int32, shape, 1),
                    1.0, 0.0)
    n = shape[0]
    ms = [-jnp.where(levels[0], a, 0.0) for a in a_mats]
    for k, lvl in enumerate(levels[1:], start=1):
        b = 1 << k
        tbs = [(m + eye).astype(BF16) for m in ms]
        es = [jnp.where(lvl, a, 0.0).astype(BF16) for a in a_mats]
        if b % BF16_ROWS:
            inner = [_dot(e, tb).astype(BF16) for e, tb in zip(es, tbs)]
            ms = [m - _dot(tb, x) for m, tb, x in zip(ms, tbs, inner)]
            continue
        blocks = [slice(j * b, (j + 1) * b) for j in range(n // b)]
        zero = jnp.zeros((b, n), BF16)
        live = [[j for j in range(n // b) if (j % 2 == 0) == rev] for rev in upper]
        rows = lambda x, js: jnp.concatenate([x[blocks[j], :] for j in js], axis=0)
        inner = [_dot(rows(e, js), tb).astype(BF16) for e, tb, js in zip(es, tbs, live)]
        spread = [jnp.concatenate([x[js.index(j) * b:(js.index(j) + 1) * b, :] if j in js else zero
                                   for j in range(n // b)], axis=0) for x, js in zip(inner, live)]
        upd = [_dot(rows(tb, js), x) for tb, x, js in zip(tbs, spread, live)]
        ms = [jnp.concatenate([m[blocks[j], :] - u[js.index(j) * b:(js.index(j) + 1) * b, :] if j in js
                               else m[blocks[j], :] for j in range(n // b)], axis=0)
              for m, u, js in zip(ms, upd, live)]
    return ms


def _chunk_order(n, reverse):
    return range(n - 1, -1, -1) if reverse else range(n)


def _gdn_chunk_problems(d, q_ref, k_ref, v_ref, p_ref, *, c, with_output):
    reverse = d == 1
    t = k_ref.shape[0]
    sp = p_ref[...]
    gc, tot = _chunk_sums(sp, c, reverse)
    strict, incl = _tri_masks(c, reverse)
    problems = []
    for ci in _chunk_order(t // c, reverse):
        rows = slice(ci * c, (ci + 1) * c)
        gcc = gc[rows, :]
        gct = gcc.T
        totc = tot[rows, :]
        for hh in range(GDN_HEADS):
            cb = COL_BETA + d * GDN_HEADS + hh
            cl = COL_GLOGA + d * GDN_HEADS + hh
            hs = slice(hh * GDN_DK, (hh + 1) * GDN_DK)
            kb = k_ref[rows, hs]
            kf = kb.astype(F32)
            vf = v_ref[rows, hs].astype(F32)
            gcol = gcc[:, cl:cl + 1]
            tcol = totc[:, cl:cl + 1]
            beta = sp[rows, cb:cb + 1]
            e_incl = _decay_matrix(gcol, gct[cl:cl + 1, :], incl)
            if with_output:
                qb = q_ref[rows, hs]
                scores = _dot_nt(jnp.concatenate([kb, qb], axis=0), kb)
                kk = scores[:c]
            else:
                kk = _dot_nt(kb, kb)
            pr = {"d": d, "hh": hh, "rows": rows, "hs": hs,
                  "a": kk * jnp.where(strict, e_incl, 0.0) * beta,
                  "rhs": jnp.concatenate([beta * vf, (beta * jnp.exp(gcol)) * kf], axis=1),
                  "k_w": (kf * jnp.exp(tcol - gcol)).astype(BF16),
                  "c_dec": jnp.exp(tcol[0:1, :])}
            if with_output:
                pr["q_w"] = (qb.astype(F32) * jnp.exp(gcol)).astype(BF16)
                pr["p"] = (scores[c:] * e_incl).astype(BF16)
            problems.append(pr)
    return problems


def _gdn_body(*refs, c, with_output):
    s0_ref = refs[0]
    ins = refs[1:9]
    if with_output:
        outs = refs[9:11]
        st_ref = refs[11]
    else:
        outs = (None, None)
        st_ref = refs[9]

    @pl.when(pl.program_id(1) == 0)
    def _():
        st_ref[...] = s0_ref[...]

    per_dir = [_gdn_chunk_problems(d, *ins[4 * d:4 * d + 4], c=c, with_output=with_output) for d in range(2)]
    n_steps = len(per_dir[0]) // GDN_HEADS
    problems = per_dir[0] + per_dir[1]
    minv = _unit_triangular_inverses_minus_eye([pr["a"] for pr in problems], [pr["d"] == 1 for pr in problems],
                                               _merge_level_masks(c))
    for pr, m in zip(problems, minv):
        x = pr["rhs"] + _dot(m.astype(BF16), pr["rhs"].astype(BF16))
        pr["w_v"] = x[:, :GDN_DV]
        pr["w_k"] = x[:, GDN_DV:].astype(BF16)
    state = {(d, hh): st_ref[d, hh] for d in range(2) for hh in range(GDN_HEADS)}
    for step in range(n_steps):
        now = [pr for prs in per_dir for pr in prs[step * GDN_HEADS:(step + 1) * GDN_HEADS]]
        sbs = [state[pr["d"], pr["hh"]].astype(BF16) for pr in now]
        us = [pr["w_v"] - _dot(pr["w_k"], sb) for pr, sb in zip(now, sbs)]
        ubs = [u.astype(BF16) for u in us]
        if with_output:
            for pr, sb, ub in zip(now, sbs, ubs):
                outs[pr["d"]][pr["rows"], pr["hs"]] = _dot(jnp.concatenate([pr["q_w"], pr["p"]], axis=1),
                                                           jnp.concatenate([sb, ub], axis=0))
        for pr, ub in zip(now, ubs):
            key = (pr["d"], pr["hh"])
            state[key] = pr["c_dec"] * state[key] + _dot_tn(pr["k_w"], ub)
    for (d, hh), s in state.items():
        st_ref[d, hh] = s


def _ret_direction(d, q_ref, k_ref, v_ref, lg_ref, o_ref, st_ref, *, c, with_output):
    reverse = d == 1
    t = q_ref.shape[0]
    r = lax.broadcasted_iota(jnp.int32, (c, c), 0)
    col = lax.broadcasted_iota(jnp.int32, (c, c), 1)
    dist = ((col - r) if reverse else (r - col)).astype(F32)
    incl = dist >= 0.0
    pos = lax.broadcasted_iota(jnp.int32, (c, 1), 0).astype(F32)
    steps = (float(c) - pos) if reverse else (pos + 1.0)
    problems, c_decs, q_dec_of = [], [], []
    for hh in range(RET_HEADS):
        hs = slice(hh * RET_DK, (hh + 1) * RET_DK)
        lg = lg_ref[d, hh]
        v_dec = jnp.exp((float(c) - steps) * lg)
        c_decs.append(jnp.exp(jnp.full((1, 1), float(c), F32) * lg))
        if with_output:
            dmat = jnp.where(incl, jnp.exp(jnp.where(incl, dist * lg, 0.0)), 0.0)
            q_dec_of.append(jnp.exp(steps * lg))
        for ci in _chunk_order(t // c, reverse):
            rows = slice(ci * c, (ci + 1) * c)
            kb = k_ref[rows, hs]
            vb = v_ref[rows, hs]
            pr = {"rows": rows, "hs": hs, "inc": _dot_tn(kb, (vb.astype(F32) * v_dec).astype(BF16))}
            if with_output:
                qb = q_ref[rows, hs]
                pr["qb"] = qb
                pr["intra"] = _dot((_dot_nt(qb, kb) * dmat).astype(BF16), vb)
            problems.append(pr)
    n_steps = t // c
    for hh in range(RET_HEADS):
        s = st_ref[d, hh]
        for pr in problems[hh * n_steps:(hh + 1) * n_steps]:
            if with_output:
                o_ref[pr["rows"], pr["hs"]] = pr["intra"] + _dot(pr["qb"], s.astype(BF16)) * q_dec_of[hh]
            s = c_decs[hh] * s + pr["inc"]
        st_ref[d, hh] = s


def _ret_body(*refs, c, with_output):
    lg_ref, s0_ref = refs[0], refs[1]
    ins = refs[2:8]
    if with_output:
        outs = refs[8:10]
        st_ref = refs[10]
    else:
        outs = (None, None)
        st_ref = refs[8]

    @pl.when(pl.program_id(1) == 0)
    def _():
        st_ref[...] = s0_ref[...]

    for d in range(2):
        q_ref, k_ref, v_ref = ins[3 * d:3 * d + 3]
        _ret_direction(d, q_ref, k_ref, v_ref, lg_ref, outs[d], st_ref, c=c, with_output=with_output)


def _ssd_chunk_problems(d, sel_ref, x_ref, b_ref, c_ref, p_ref, o_ref, *, c, with_output):
    reverse = d == 1
    t = x_ref.shape[0]
    sp = p_ref[...]
    gc, tot = _chunk_sums(sp, c, reverse)
    col = lax.broadcasted_iota(jnp.int32, sp.shape, 1)
    lo = COL_SLOGA + d * SSD_HEADS
    mine = jnp.where(col >= lo, jnp.where(col < lo + SSD_HEADS, 1.0, 0.0), 0.0)
    gcm = gc * mine
    totm = tot * mine
    sel_d = sel_ref[d, 0]
    sel_l = sel_ref[d, 1]
    xd = x_ref[...].astype(F32) * _dot(sp.astype(BF16), sel_d)
    xdb = xd.astype(BF16)
    n_chunks = t // c
    stack = [jnp.exp(totm - gcm).astype(BF16)]
    if with_output:
        stack.append(jnp.exp(gcm).astype(BF16))
    tot_base = len(stack) * t
    for ci in range(n_chunks):
        stack += list(_split3(jnp.exp(totm[ci * c:ci * c + BF16_ROWS, :])))
    expanded = _dot(jnp.concatenate(stack, axis=0), sel_l)
    vw = (xd * expanded[0:t, :]).astype(BF16)
    if with_output:
        e1x = expanded[t:2 * t, :]
        _, incl = _tri_masks(c, reverse)
        lane = lax.broadcasted_iota(jnp.int32, (c, 2 * SSD_HEADDIM), 1)
        first_head = lane < SSD_HEADDIM
    gw = SSD_HG * SSD_HEADDIM
    problems = []
    for ci in _chunk_order(t // c, reverse):
        rows = slice(ci * c, (ci + 1) * c)
        gcc = gc[rows, :]
        gct = gcc.T
        base = tot_base + 3 * BF16_ROWS * ci
        c_dec = (expanded[base:base + 1, :] + expanded[base + BF16_ROWS:base + BF16_ROWS + 1, :]
                 + expanded[base + 2 * BF16_ROWS:base + 2 * BF16_ROWS + 1, :])
        for gg in range(SSD_GROUPS):
            gs = slice(gg * SSD_STATE, (gg + 1) * SSD_STATE)
            gcols = slice(gg * gw, (gg + 1) * gw)
            bb = b_ref[rows, gs]
            pr = {"d": d, "gg": gg, "rows": rows, "gcols": gcols,
                  "inc": _dot_tn(bb, vw[rows, gcols]), "c_dec": c_dec[:, gcols]}
            if with_output:
                cb = c_ref[rows, gs]
                scores = _dot_nt(cb, bb)
                for pair in range(SSD_HG // 2):
                    h0 = gg * SSD_HG + 2 * pair
                    cols = slice(h0 * SSD_HEADDIM, (h0 + 2) * SSD_HEADDIM)
                    ms = []
                    for hd in (h0, h0 + 1):
                        cl = lo + hd
                        dmat = _decay_matrix(gcc[:, cl:cl + 1], gct[cl:cl + 1, :], incl)
                        ms.append((scores * dmat).astype(BF16))
                    xp = xdb[rows, cols]
                    zero = jnp.zeros_like(xp)
                    rhs = jnp.concatenate([jnp.where(first_head, xp, zero), jnp.where(first_head, zero, xp)], axis=0)
                    o_ref[rows, cols] = _dot(jnp.concatenate(ms, axis=1), rhs)
                pr["cb"] = cb
                pr["q_dec"] = e1x[rows, gcols]
            problems.append(pr)
    return problems


def _ssd_body(*refs, c, with_output):
    sel_ref, s0_ref = refs[0], refs[1]
    ins = refs[2:10]
    if with_output:
        outs = refs[10:12]
        st_ref = refs[12]
    else:
        outs = (None, None)
        st_ref = refs[10]

    @pl.when(pl.program_id(1) == 0)
    def _():
        st_ref[...] = s0_ref[...]

    per_dir = [_ssd_chunk_problems(d, sel_ref, *ins[4 * d:4 * d + 4], outs[d], c=c, with_output=with_output)
               for d in range(2)]
    n_steps = len(per_dir[0]) // SSD_GROUPS
    state = {(d, gg): st_ref[d, gg] for d in range(2) for gg in range(SSD_GROUPS)}
    for step in range(n_steps):
        for prs in per_dir:
            for pr in prs[step * SSD_GROUPS:(step + 1) * SSD_GROUPS]:
                key = (pr["d"], pr["gg"])
                if with_output:
                    o_ref = outs[pr["d"]]
                    inter = _dot(pr["cb"], state[key].astype(BF16))
                    o_ref[pr["rows"], pr["gcols"]] = o_ref[pr["rows"], pr["gcols"]] + inter * pr["q_dec"]
                state[key] = pr["c_dec"] * state[key] + pr["inc"]
    for (d, gg), s in state.items():
        st_ref[d, gg] = s


def ssd_head_selectors():
    sel = np.zeros((2, 2, SMALL_W, SSD_DINNER), np.float32)
    heads = np.arange(SSD_DINNER) // SSD_HEADDIM
    for d in range(2):
        sel[d, 0, COL_DELTA + d * SSD_HEADS + heads, np.arange(SSD_DINNER)] = 1.0
        sel[d, 1, COL_SLOGA + d * SSD_HEADS + heads, np.arange(SSD_DINNER)] = 1.0
    return jnp.asarray(sel, BF16)


def _scan_call(body, name, s0, arrays, out_width, with_output, extra_args=(), extra_specs=(), block=SCAN_BLOCK):
    b_, seq, _ = arrays[0].shape
    t = min(block, seq)
    c = min(SCAN_CHUNK, t)
    nt = seq // t
    fwd = lambda w: pl.BlockSpec((None, t, w), lambda b, i: (b, i, 0))
    bwd = lambda w: pl.BlockSpec((None, t, w), lambda b, i: (b, nt - 1 - i, 0))
    st_spec = pl.BlockSpec((None,) + s0.shape[1:], lambda b, i: (b,) + (0,) * (s0.ndim - 1))
    in_specs = list(extra_specs) + [st_spec]
    in_specs += [fwd(a.shape[-1]) for a in arrays] + [bwd(a.shape[-1]) for a in arrays]
    out_specs, out_shape = [], []
    if with_output:
        out_specs += [fwd(out_width), bwd(out_width)]
        out_shape += [jax.ShapeDtypeStruct((b_, seq, out_width), F32)] * 2
    out_specs.append(st_spec)
    out_shape.append(jax.ShapeDtypeStruct(s0.shape, F32))
    res = pl.pallas_call(
        functools.partial(body, c=c, with_output=with_output),
        grid=(b_, nt),
        in_specs=in_specs,
        out_specs=out_specs,
        out_shape=out_shape,
        compiler_params=_cparams(2),
        name=name,
    )(*extra_args, s0, *arrays, *arrays)
    if with_output:
        return res[0], res[1], res[2]
    return None, None, res[0]


def token_mixers(proj, prep, states, lp, with_output):
    gdn_s0, ret_s0, ssd_s0 = states
    a_f, a_b, gdn_s = _scan_call(_gdn_body, "gdn_scan", gdn_s0,
                                 [prep["gq"], prep["gk"], prep["gv"], prep["smallp"]],
                                 GDN_HEADS * GDN_DV, with_output)
    ret_lg = -jnp.exp(lp["ret_decay"].astype(F32))
    sel = ssd_head_selectors()
    b_f, b_b, ret_s = _scan_call(_ret_body, "ret_scan", ret_s0,
                                 [prep["rq"], prep["rk"], proj["r_v"]],
                                 RET_HEADS * RET_DV, with_output, block=RET_SCAN_BLOCK, extra_args=(ret_lg,),
                                 extra_specs=(pl.BlockSpec(memory_space=pltpu.SMEM),))
    c_f, c_b, ssd_s = _scan_call(_ssd_body, "ssd_scan", ssd_s0,
                                 [prep["sx"], prep["sb"], prep["sc"], prep["smallp"]],
                                 SSD_DINNER, with_output, extra_args=(sel,),
                                 extra_specs=(_resident(sel.shape),))
    o = None
    if with_output:
        o = {"a_f": a_f, "a_b": a_b, "b_f": b_f, "b_b": b_b, "c_f": c_f, "c_b": c_b}
    return o, (gdn_s, ret_s, ssd_s)


def zero_states(b_):
    return (jnp.zeros((b_, 2, GDN_HEADS, GDN_DK, GDN_DV), F32),
            jnp.zeros((b_, 2, RET_HEADS, RET_DK, RET_DV), F32),
            jnp.zeros((b_, 2, SSD_GROUPS, SSD_STATE, SSD_HG * SSD_HEADDIM), F32))


def _merge_body(h_ref, mod_ref, oaf_ref, oab_ref, obf_ref, obb_ref, ocf_ref, ocb_ref, sx_ref,
                gz_ref, rg_ref, sz_ref, brg_ref, gng_ref, rng_ref, sd_ref, sng_ref,
                wa_ref, wb_ref, wc_ref, wo_ref, lng_ref, lnb_ref, o_ref, *, alpha):
    d = h_ref.shape[-1]
    oa = oaf_ref[...] + oab_ref[...]
    gz = gz_ref[...].astype(F32)
    ya = []
    for hh in range(GDN_HEADS):
        x = oa[:, hh * GDN_DV:(hh + 1) * GDN_DV]
        ms = jnp.mean(x * x, axis=-1, keepdims=True)
        ya.append(x * lax.rsqrt(ms + RMS_EPS) * gng_ref[...] * gz[:, hh * GDN_DV:(hh + 1) * GDN_DV])
    ya = jnp.concatenate(ya, axis=-1).astype(BF16)
    ob = obf_ref[...] + obb_ref[...]
    rg = rg_ref[...].astype(F32)
    yb = []
    for hh in range(RET_HEADS):
        sl = slice(hh * RET_DV, (hh + 1) * RET_DV)
        x = ob[:, sl]
        mu = jnp.mean(x, axis=-1, keepdims=True)
        xc = x - mu
        var = jnp.mean(xc * xc, axis=-1, keepdims=True)
        yb.append(xc * lax.rsqrt(var + LN_EPS) * rng_ref[:, sl] * rg[:, sl])
    yb = jnp.concatenate(yb, axis=-1).astype(BF16)
    oc = (ocf_ref[...] + ocb_ref[...] + sd_ref[...] * sx_ref[...].astype(F32)) * sz_ref[...].astype(F32)
    gw = SSD_DINNER // SSD_GROUPS
    yc = []
    for gg in range(SSD_GROUPS):
        sl = slice(gg * gw, (gg + 1) * gw)
        x = oc[:, sl]
        ms = jnp.mean(x * x, axis=-1, keepdims=True)
        yc.append(x * lax.rsqrt(ms + RMS_EPS) * sng_ref[:, sl])
    yc = jnp.concatenate(yc, axis=-1).astype(BF16)
    gates = brg_ref[...].astype(F32)
    merged = (gates[:, 0:d] * _dot(ya, wa_ref[...])
              + gates[:, d:2 * d] * _dot(yb, wb_ref[...])
              + gates[:, 2 * d:3 * d] * _dot(yc, wc_ref[...]))
    mix = _dot(merged.astype(BF16), wo_ref[...])
    h = h_ref[...]
    y = alpha * h + mod_ref[5:6, :] * mix
    o_ref[...] = _layer_norm_rows(y, lng_ref[...], lnb_ref[...])


def mixer_merge(h, mod, o, proj, sx, lp, ln_g, ln_b, *, alpha, tm):
    b_, seq, d = h.shape
    tm = min(tm, seq)

    def tok(w):
        return pl.BlockSpec((None, tm, w), lambda b, i: (b, i, 0))

    ssd_d_cols = jnp.repeat(lp["ssd_d"].astype(F32), SSD_HEADDIM).reshape(1, SSD_DINNER)
    args = [h, mod, o["a_f"], o["a_b"], o["b_f"], o["b_b"], o["c_f"], o["c_b"], sx,
            proj["g_z"], proj["r_g"], proj["s_z"], proj["br_gate"],
            lp["gdn_norm_g"].astype(F32).reshape(1, GDN_DV),
            lp["ret_norm_g"].astype(F32).reshape(1, RET_HEADS * RET_DV),
            ssd_d_cols, lp["ssd_norm_g"].astype(F32).reshape(1, SSD_DINNER),
            lp["w_br_a"], lp["w_br_b"], lp["w_br_c"], lp["w_out"], ln_g.reshape(1, d), ln_b.reshape(1, d)]
    in_specs = [tok(d), pl.BlockSpec((None, N_ADA, d), lambda b, i: (b, 0, 0)),
                tok(GDN_HEADS * GDN_DV), tok(GDN_HEADS * GDN_DV),
                tok(RET_HEADS * RET_DV), tok(RET_HEADS * RET_DV),
                tok(SSD_DINNER), tok(SSD_DINNER), tok(SSD_DINNER),
                tok(GDN_HEADS * GDN_DV), tok(RET_HEADS * RET_DV), tok(SSD_DINNER), tok(N_BRANCH * d)]
    lead = (lp["layer"],)
    in_specs += [_resident_slice(a, lead) if a.ndim == 3 else _resident(a.shape) for a in args[13:]]
    return pl.pallas_call(
        functools.partial(_merge_body, alpha=alpha),
        grid=(b_, seq // tm),
        in_specs=in_specs,
        out_specs=tok(d),
        out_shape=jax.ShapeDtypeStruct(h.shape, F32),
        compiler_params=_cparams(2),
        name="mixer_merge",
    )(*args)


def kernel(x, c, ctx, c_ctx, ada_w, ada_b, ln_g, ln_b, ffn_w13, ffn_w2, mix_w_in,
           gdn_conv_w, gdn_a_log, gdn_dt_bias, gdn_norm_g, ret_decay, ret_norm_g,
           ssd_conv_w, ssd_conv_b, ssd_a_log, ssd_dt_bias, ssd_d, ssd_norm_g,
           w_br_a, w_br_b, w_br_c, mix_w_out):
    depth = ada_w.shape[0]
    alpha = float((2 * depth) ** 0.25)
    b_, seq, d = x.shape
    rope2 = rope_tables(seq)
    c_rows = jnp.concatenate([c, c_ctx[None, :]], axis=0)
    h, hc = x, ctx
    w13, w2 = ffn_w13.astype(BF16), ffn_w2.astype(BF16)
    w_in_r = reorder_w_in(mix_w_in, d)
    w_branch = {'w_br_a': w_br_a.astype(BF16), 'w_br_b': w_br_b.astype(BF16),
                'w_br_c': w_br_c.astype(BF16), 'w_out': mix_w_out.astype(BF16)}
    ada_b3 = ada_b.reshape(depth, 1, -1)
    for i in range(depth):
        last = i == depth - 1
        mod_all = ada_modulation(c_rows, ada_w, ada_b3, i)
        mod = mod_all[:b_].reshape(b_, N_ADA, d)
        mod_c = jnp.broadcast_to(mod_all[b_:].reshape(1, N_ADA, d), (b_, N_ADA, d))
        lp = {'layer': i, 'gdn_conv_w': gdn_conv_w[i], 'gdn_a_log': gdn_a_log[i],
              'gdn_dt_bias': gdn_dt_bias[i], 'gdn_norm_g': gdn_norm_g[i], 'ret_decay': ret_decay[i],
              'ret_norm_g': ret_norm_g[i], 'ssd_conv_w': ssd_conv_w[i], 'ssd_conv_b': ssd_conv_b[i],
              'ssd_a_log': ssd_a_log[i], 'ssd_dt_bias': ssd_dt_bias[i], 'ssd_d': ssd_d[i],
              'ssd_norm_g': ssd_norm_g[i], **w_branch}
        ffn = functools.partial(ffn_sublayer, alpha=alpha, tm=1024)
        h = ffn(h, mod, w13, w2, (i, 0), ln_g[i, 0], ln_b[i, 0], mod_base=0)
        hc = ffn(hc, mod_c, w13, w2, (i, 0), ln_g[i, 0], ln_b[i, 0], mod_base=0)
        proj_c, prep_c = mixer_in_projection_prep(hc, mod_c, w_in_r, None, lp, tm=256)
        o_c, ctx_states = token_mixers(proj_c, prep_c, zero_states(b_), lp, not last)
        proj, prep = mixer_in_projection_prep(h, mod, w_in_r, rope2, lp, tm=256)
        o_l, _ = token_mixers(proj, prep, ctx_states, lp, True)
        h = mixer_merge(h, mod, o_l, proj, prep["sx"], lp, ln_g[i, 1], ln_b[i, 1], alpha=alpha, tm=256)
        h = ffn(h, mod, w13, w2, (i, 1), ln_g[i, 2], ln_b[i, 2], mod_base=6)
        if not last:
            hc = mixer_merge(hc, mod_c, o_c, proj_c, prep_c["sx"], lp, ln_g[i, 1], ln_b[i, 1],
                             alpha=alpha, tm=256)
            hc = ffn(hc, mod_c, w13, w2, (i, 1), ln_g[i, 2], ln_b[i, 2], mod_base=6)
    return h
```

```python
import functools

import numpy as np
import jax
import jax.numpy as jnp
from jax import lax
from jax.experimental import pallas as pl
from jax.experimental.pallas import tpu as pltpu

F32 = jnp.float32
BF16 = jnp.bfloat16

GRID_W = 64
CONV_K = 5
GDN_HEADS = 4
GDN_DK = 128
GDN_DV = 128
RET_HEADS = 4
RET_DK = 128
RET_DV = 128
SSD_HEADS = 16
SSD_HEADDIM = 64
SSD_GROUPS = 2
SSD_STATE = 128
SSD_DINNER = SSD_HEADS * SSD_HEADDIM
SSD_HG = SSD_HEADS // SSD_GROUPS
N_BRANCH = 3
N_ADA = 9
ROPE_BASE = 10000.0
GDN_QKV = GDN_HEADS * (2 * GDN_DK + GDN_DV)
SSD_XBC = SSD_DINNER + 2 * SSD_GROUPS * SSD_STATE
LN_EPS = 1e-5
RMS_EPS = 1e-6
L2_EPS = 1e-6

VMEM_LIMIT_BYTES = 56 * 1024 * 1024
LANE = 128
SMALL_W = LANE
BF16_ROWS = 16
CONV_ROWS = 128
FFN_ROWS = 256
INPROJ_ROWS = 256
SCAN_CHUNK = 128
SCAN_BLOCK = 256
RET_SCAN_BLOCK = 512

COL_BETA = 0
COL_GLOGA = 2 * GDN_HEADS
COL_DELTA = 4 * GDN_HEADS
COL_SLOGA = COL_DELTA + 2 * SSD_HEADS
COL_END = COL_SLOGA + 2 * SSD_HEADS


def _cparams(n_axes):
    return pltpu.CompilerParams(dimension_semantics=("arbitrary",) * n_axes,
                                vmem_limit_bytes=VMEM_LIMIT_BYTES)


def _resident(shape):
    nd = len(shape)
    return pl.BlockSpec(shape, lambda *_: (0,) * nd, pipeline_mode=pl.Buffered(1))


def _resident_slice(arr, lead):
    tail = tuple(arr.shape[len(lead):])
    return pl.BlockSpec((None,) * len(lead) + tail, lambda *_: tuple(lead) + (0,) * len(tail),
                        pipeline_mode=pl.Buffered(1))


def _layer_norm_rows(y, g, b):
    mu = jnp.mean(y, axis=-1, keepdims=True)
    yc = y - mu
    var = jnp.mean(yc * yc, axis=-1, keepdims=True)
    return yc * lax.rsqrt(var + LN_EPS) * g + b


def _silu(x):
    return x * jax.nn.sigmoid(x)


def _softplus(x):
    return jnp.maximum(x, 0.0) + jnp.log1p(jnp.exp(-jnp.abs(x)))


def _dot(a, b):
    return jnp.dot(a, b, preferred_element_type=F32)


def _dot_nt(a, b):
    return lax.dot_general(a, b, (((1,), (1,)), ((), ())), preferred_element_type=F32)


def _dot_tn(a, b):
    return lax.dot_general(a, b, (((0,), (0,)), ((), ())), preferred_element_type=F32)


def _ada_body(c_ref, w_ref, b_ref, o_ref):
    s = _silu(c_ref[...]).astype(BF16)
    o_ref[...] = _dot(s, w_ref[...].astype(BF16)) + b_ref[...]


def ada_modulation(c_rows, w, b, layer):
    r, d = c_rows.shape
    n = w.shape[-1]
    tn = 1024
    return pl.pallas_call(
        _ada_body,
        grid=(n // tn,),
        in_specs=[pl.BlockSpec((r, d), lambda j: (0, 0)),
                  pl.BlockSpec((None, d, tn), lambda j: (layer, 0, j)),
                  pl.BlockSpec((None, 1, tn), lambda j: (layer, 0, j))],
        out_specs=pl.BlockSpec((r, tn), lambda j: (0, j)),
        out_shape=jax.ShapeDtypeStruct((r, n), F32),
        compiler_params=_cparams(1),
        name="ada_modulation",
    )(c_rows, w, b)


def _ffn_body(h_ref, mod_ref, w13_ref, w2_ref, lng_ref, lnb_ref, o_ref, *, alpha, ff, chunks, mod_base):
    shift = mod_ref[mod_base:mod_base + 1, :]
    scale = mod_ref[mod_base + 1:mod_base + 2, :]
    gate = mod_ref[mod_base + 2:mod_base + 3, :]
    tm = h_ref.shape[0]
    rb = min(tm, FFN_ROWS)
    for r0 in range(0, tm, rb):
        h = h_ref[r0:r0 + rb, :]
        u = (h * (1.0 + scale) + shift).astype(BF16)
        acc = None
        for c0, c1 in chunks:
            a = _dot(u, w13_ref[:, c0:c1])
            b = _dot(u, w13_ref[:, ff + c0:ff + c1])
            g = (_silu(a) * b).astype(BF16)
            p = _dot(g, w2_ref[c0:c1, :])
            acc = p if acc is None else acc + p
        y = alpha * h + (0.5 * gate) * acc
        o_ref[r0:r0 + rb, :] = _layer_norm_rows(y, lng_ref[...], lnb_ref[...])


def _ff_chunks(ff, width=1024):
    return tuple((c0, min(c0 + width, ff)) for c0 in range(0, ff, width))


def ffn_sublayer(h, mod, w13, w2, which, ln_g, ln_b, *, alpha, mod_base, tm):
    b_, seq, d = h.shape
    ff = w2.shape[-2]
    tm = min(tm, seq)
    body = functools.partial(_ffn_body, alpha=alpha, ff=ff, chunks=_ff_chunks(ff), mod_base=mod_base)
    return pl.pallas_call(
        body,
        grid=(b_, seq // tm),
        in_specs=[pl.BlockSpec((None, tm, d), lambda b, i: (b, i, 0)),
                  pl.BlockSpec((None, N_ADA, d), lambda b, i: (b, 0, 0)),
                  _resident_slice(w13, which), _resident_slice(w2, which),
                  _resident((1, d)), _resident((1, d))],
        out_specs=pl.BlockSpec((None, tm, d), lambda b, i: (b, i, 0)),
        out_shape=jax.ShapeDtypeStruct(h.shape, F32),
        compiler_params=_cparams(2),
        name="ffn_sublayer",
    )(h, mod, w13, w2, ln_g.reshape(1, d), ln_b.reshape(1, d))


W_IN_PIECES = ((("g_qkv", GDN_QKV), ("g_z", GDN_HEADS * GDN_DV)),
               (("r_q", RET_HEADS * RET_DK), ("r_k", RET_HEADS * RET_DK), ("r_v", RET_HEADS * RET_DV),
                ("r_g", RET_HEADS * RET_DV), ("s_z", SSD_DINNER), ("s_xbc", SSD_XBC)),
               (("br_gate", None),),
               (("small", SMALL_W),))


def split_w_in(w_in, d):
    sizes = (GDN_QKV, GDN_HEADS * GDN_DV, 2 * GDN_HEADS, 2 * GDN_HEADS,
             RET_HEADS * RET_DK, RET_HEADS * RET_DK, RET_HEADS * RET_DV, RET_HEADS * RET_DV,
             SSD_DINNER, SSD_XBC, 2 * SSD_HEADS, N_BRANCH * d)
    offs = np.concatenate([[0], np.cumsum(sizes)])
    w_in = w_in.astype(BF16)
    n_small = 4 * GDN_HEADS + 2 * SSD_HEADS
    pad = jnp.zeros(w_in.shape[:-1] + (SMALL_W - n_small,), w_in.dtype)
    small = jnp.concatenate([w_in[..., offs[2]:offs[4]], w_in[..., offs[10]:offs[11]], pad], axis=-1)
    return (w_in[..., offs[0]:offs[2]], w_in[..., offs[4]:offs[10]], w_in[..., offs[11]:offs[12]], small)


F32_ROWS = 8
def _w_in_offsets(d):
    offs = {}
    for piece, groups in enumerate(W_IN_PIECES):
        c0 = 0
        for name, width in groups:
            width = N_BRANCH * d if width is None else width
            offs[name] = (piece, c0, c0 + width)
            c0 += width
    return offs


def _conv_silu_rows(raw, tm, w_ref, b_ref, first, last):
    h8 = F32_ROWS
    xx = jnp.concatenate([jnp.where(first, 0.0, raw[tm:tm + h8, :]), raw[0:tm, :],
                          jnp.where(last, 0.0, raw[tm + h8:tm + 2 * h8, :])], axis=0).astype(BF16)
    rb = min(tm, CONV_ROWS)
    win = rb + 2 * h8
    r = lax.broadcasted_iota(jnp.int32, (rb, CONV_K * win), 0)
    col = lax.broadcasted_iota(jnp.int32, (rb, CONV_K * win), 1)
    tap = jnp.zeros_like(col)
    for k in range(1, CONV_K):
        tap = tap + jnp.where(col >= k * win, 1, 0)
    shifts = jnp.where(col == r + tap * (win + 1) + (h8 - CONV_K // 2), 1.0, 0.0).astype(BF16)
    wb = [w_ref[k:k + 1, :].astype(BF16) for k in range(CONV_K)]
    out = []
    for blk in range(tm // rb):
        window = xx[blk * rb:blk * rb + win, :]
        out.append(_dot(shifts, jnp.concatenate([window * wk for wk in wb], axis=0)))
    acc = out[0] if len(out) == 1 else jnp.concatenate(out, axis=0)
    if b_ref is not None:
        acc = acc + b_ref[...]
    return _silu(acc)


def _inproj_prep_body(*refs, use_rope, offs):
    hp_ref, h_ref, hn_ref, mod_ref = refs[:4]
    w_refs = refs[4:4 + len(W_IN_PIECES)]
    k = 4 + len(W_IN_PIECES)
    if use_rope:
        cos_ref, sin_ref = refs[k:k + 2]
        k += 2
    gw_ref, sw_ref, sb_ref, bias_ref, alog_ref = refs[k:k + 5]
    (gq_o, gk_o, gv_o, gz_o, rq_o, rk_o, rv_o, rg_o, sz_o, sx_o, sb_o, sc_o, brg_o, sm_o) = refs[k + 5:]
    i = pl.program_id(1)
    tile_first = i == 0
    tile_last = i == pl.num_programs(1) - 1
    tile = h_ref.shape[0]
    tm = min(tile, INPROJ_ROWS)
    h8 = F32_ROWS
    shift = mod_ref[3:4, :]
    scale = mod_ref[4:5, :]

    def proj(lhs, name):
        piece, c0, c1 = offs[name]
        return _dot(lhs, w_refs[piece][:, c0:c1])

    for r0 in range(0, tile, tm):
        rows = slice(r0, r0 + tm)
        before = hp_ref[...] if r0 == 0 else h_ref[r0 - h8:r0, :]
        after = hn_ref[...] if r0 + tm == tile else h_ref[r0 + tm:r0 + tm + h8, :]
        first = tile_first if r0 == 0 else False
        last = tile_last if r0 + tm == tile else False
        u_ext = (jnp.concatenate([h_ref[rows, :], before, after], axis=0) * (1.0 + scale) + shift).astype(BF16)
        u = u_ext[0:tm, :]
        raw_g = proj(u_ext, "g_qkv")
        raw_s = proj(u_ext, "s_xbc")

        qkv = _conv_silu_rows(raw_g, tm, gw_ref, None, first, last)
        rq = proj(u, "r_q")
        rk = proj(u, "r_k") * (RET_DK ** -0.5)
        nq = GDN_HEADS * GDN_DK
        for hh in range(GDN_HEADS):
            sl = slice(hh * GDN_DK, (hh + 1) * GDN_DK)
            q = qkv[:, sl]
            kk = qkv[:, nq + hh * GDN_DK: nq + (hh + 1) * GDN_DK]
            q = q * lax.rsqrt(jnp.sum(q * q, axis=-1, keepdims=True) + L2_EPS) * (GDN_DK ** -0.5)
            kk = kk * lax.rsqrt(jnp.sum(kk * kk, axis=-1, keepdims=True) + L2_EPS)
            gq_o[rows, sl] = q.astype(BF16)
            gk_o[rows, sl] = kk.astype(BF16)
        gv_o[rows, :] = qkv[:, 2 * nq:].astype(BF16)

        xbc = _conv_silu_rows(raw_s, tm, sw_ref, sb_ref, first, last)
        sm = proj(u, "small")
        rv_o[rows, :] = proj(u, "r_v").astype(BF16)
        gz_o[rows, :] = _silu(proj(u, "g_z")).astype(BF16)
        sx_o[rows, :] = xbc[:, :SSD_DINNER].astype(BF16)
        sb_o[rows, :] = xbc[:, SSD_DINNER:SSD_DINNER + SSD_GROUPS * SSD_STATE].astype(BF16)
        sc_o[rows, :] = xbc[:, SSD_DINNER + SSD_GROUPS * SSD_STATE:].astype(BF16)

        rg_o[rows, :] = _silu(proj(u, "r_g")).astype(BF16)
        sz_o[rows, :] = _silu(proj(u, "s_z")).astype(BF16)
        if use_rope:
            cos2 = cos_ref[rows, :]
            sin2 = sin_ref[rows, :]
            for hh in range(RET_HEADS):
                sl = slice(hh * RET_DK, (hh + 1) * RET_DK)
                for src, dst in ((rq, rq_o), (rk, rk_o)):
                    xh = src[:, sl]
                    dst[rows, sl] = (xh * cos2 + pltpu.roll(xh, RET_DK // 2, axis=1) * sin2).astype(BF16)
        else:
            rq_o[rows, :] = rq.astype(BF16)
            rk_o[rows, :] = rk.astype(BF16)

        col = lax.broadcasted_iota(jnp.int32, sm.shape, 1)
        shifted = pltpu.roll(sm, COL_SLOGA - COL_DELTA, axis=1)
        xin = jnp.where(col < COL_SLOGA, sm, shifted)
        sp = _softplus(xin + bias_ref[...])
        neg_a = -jnp.exp(alog_ref[...])
        sm_o[rows, :] = jnp.where(col < COL_GLOGA, jax.nn.sigmoid(sm),
                                  jnp.where(col < COL_DELTA, neg_a * sp,
                                            jnp.where(col < COL_SLOGA, sp,
                                                      jnp.where(col < COL_END, sp * neg_a, 0.0))))
        brg_o[rows, :] = jax.nn.sigmoid(proj(u, "br_gate")).astype(BF16)


def mixer_in_projection_prep(h, mod, w_in_r, rope2, lp, *, tm):
    b_, seq, d = h.shape
    tm = min(tm, seq)
    nh = tm // F32_ROWS
    n_halo = seq // F32_ROWS
    use_rope = rope2 is not None
    offs = _w_in_offsets(d)

    def tok(w):
        return pl.BlockSpec((None, tm, w), lambda b, i: (b, i, 0))

    zeros = lambda n: jnp.zeros((n,), F32)
    bias_row = jnp.concatenate([zeros(COL_GLOGA), lp["gdn_dt_bias"].astype(F32).reshape(-1),
                                lp["ssd_dt_bias"].astype(F32).reshape(-1),
                                lp["ssd_dt_bias"].astype(F32).reshape(-1),
                                zeros(SMALL_W - COL_END)]).reshape(1, SMALL_W)
    alog_row = jnp.concatenate([zeros(COL_GLOGA), lp["gdn_a_log"].astype(F32).reshape(-1),
                                zeros(COL_SLOGA - COL_DELTA), lp["ssd_a_log"].astype(F32).reshape(-1),
                                zeros(SMALL_W - COL_END)]).reshape(1, SMALL_W)
    args = [h, h, h, mod, *w_in_r]
    in_specs = [pl.BlockSpec((None, F32_ROWS, d), lambda b, i: (b, jnp.maximum(i * nh - 1, 0), 0)),
                tok(d),
                pl.BlockSpec((None, F32_ROWS, d), lambda b, i: (b, jnp.minimum((i + 1) * nh, n_halo - 1), 0)),
                pl.BlockSpec((None, N_ADA, d), lambda b, i: (b, 0, 0)),
                *[_resident_slice(w, (lp["layer"],)) for w in w_in_r]]
    if use_rope:
        args += [rope2[0], rope2[1]]
        in_specs += [pl.BlockSpec((tm, RET_DK), lambda b, i: (i, 0))] * 2
    consts = [lp["gdn_conv_w"].astype(F32), lp["ssd_conv_w"].astype(F32),
              lp["ssd_conv_b"].astype(F32).reshape(1, SSD_XBC), bias_row, alog_row]
    args += consts
    in_specs += [_resident(a.shape) for a in consts]
    names = ("gq", "gk", "gv", "g_z", "rq", "rk", "r_v", "r_g", "s_z", "sx", "sb", "sc", "br_gate", "smallp")
    hw = GDN_HEADS * GDN_DK
    widths = (hw, hw, GDN_HEADS * GDN_DV, GDN_HEADS * GDN_DV, RET_HEADS * RET_DK, RET_HEADS * RET_DK,
              RET_HEADS * RET_DV, RET_HEADS * RET_DV, SSD_DINNER, SSD_DINNER, SSD_GROUPS * SSD_STATE,
              SSD_GROUPS * SSD_STATE, N_BRANCH * d, SMALL_W)
    dtypes = (BF16,) * 13 + (F32,)
    outs = pl.pallas_call(
        functools.partial(_inproj_prep_body, use_rope=use_rope, offs=offs),
        grid=(b_, seq // tm),
        in_specs=in_specs,
        out_specs=[tok(w) for w in widths],
        out_shape=[jax.ShapeDtypeStruct((b_, seq, w), t) for w, t in zip(widths, dtypes)],
        compiler_params=_cparams(2),
        name="mixer_in_projection_prep",
    )(*args)
    res = dict(zip(names, outs))
    gates = {n: res[n] for n in ("g_z", "r_v", "r_g", "s_z", "br_gate")}
    prep = {n: res[n] for n in ("gq", "gk", "gv", "rq", "rk", "sx", "sb", "sc", "smallp")}
    return gates, prep


def rope_tables(seq_len):
    rows = seq_len // GRID_W
    row_id = jnp.repeat(jnp.arange(rows, dtype=F32), GRID_W)
    col_id = jnp.tile(jnp.arange(GRID_W, dtype=F32), rows)
    n_freq = RET_DK // 4
    inv_freq = ROPE_BASE ** (-jnp.arange(n_freq, dtype=F32) / n_freq)
    ang = jnp.concatenate([row_id[:, None] * inv_freq, col_id[:, None] * inv_freq], axis=-1)
    cos, sin = jnp.cos(ang), jnp.sin(ang)
    return jnp.concatenate([cos, cos], axis=-1), jnp.concatenate([-sin, sin], axis=-1)


def _tri_masks(c, reverse):
    r = lax.broadcasted_iota(jnp.int32, (c, c), 0)
    col = lax.broadcasted_iota(jnp.int32, (c, c), 1)
    if reverse:
        return col > r, col >= r
    return col < r, col <= r


def _split3(x):
    hi = x.astype(BF16)
    r1 = x - hi.astype(F32)
    mid = r1.astype(BF16)
    lo = (r1 - mid.astype(F32)).astype(BF16)
    return hi, mid, lo


def _chunk_sums(x, c, reverse):
    t, w = x.shape
    r = lax.broadcasted_iota(jnp.int32, (c, c), 0)
    col = lax.broadcasted_iota(jnp.int32, (c, c), 1)
    m_cum = jnp.where((col >= r) if reverse else (col <= r), 1.0, 0.0).astype(BF16)
    cums, tots = [], []
    for ci in range(t // c):
        three = _dot(m_cum, jnp.concatenate(_split3(x[ci * c:(ci + 1) * c, :]), axis=1))
        cum = three[:, 0:w] + three[:, w:2 * w] + three[:, 2 * w:3 * w]
        end = 0 if reverse else c - 1
        cums.append(cum)
        tots.append(jnp.broadcast_to(cum[end:end + 1, :], (c, w)))
    if len(cums) == 1:
        return cums[0], tots[0]
    return jnp.concatenate(cums, axis=0), jnp.concatenate(tots, axis=0)


def _decay_matrix(gcol, grow, incl):
    return jnp.where(incl, jnp.exp(jnp.where(incl, gcol - grow, 0.0)), 0.0)


def _merge_level_masks(c):
    r = lax.broadcasted_iota(jnp.int32, (c, c), 0)
    col = lax.broadcasted_iota(jnp.int32, (c, c), 1)
    x = lax.bitwise_xor(r, col)
    return [lax.shift_right_logical(x, k) == 1 for k in range(int(np.log2(c)))]


def _unit_triangular_inverses_minus_eye(a_mats, upper, levels):
    shape = a_mats[0].shape
    eye = jnp.where(lax.broadcasted_iota(jnp.int32, shape, 0) == lax.broadcasted_iota(jnp.int32, shape, 1),
                    1.0, 0.0)
    n = shape[0]
    ms = [-jnp.where(levels[0], a, 0.0) for a in a_mats]
    for k, lvl in enumerate(levels[1:], start=1):
        b = 1 << k
        tbs = [(m + eye).astype(BF16) for m in ms]
        es = [jnp.where(lvl, a, 0.0).astype(BF16) for a in a_mats]
        if b % BF16_ROWS:
            inner = [_dot(e, tb).astype(BF16) for e, tb in zip(es, tbs)]
            ms = [m - _dot(tb, x) for m, tb, x in zip(ms, tbs, inner)]
            continue
        blocks = [slice(j * b, (j + 1) * b) for j in range(n // b)]
        zero = jnp.zeros((b, n), BF16)
        live = [[j for j in range(n // b) if (j % 2 == 0) == rev] for rev in upper]
        rows = lambda x, js: jnp.concatenate([x[blocks[j], :] for j in js], axis=0)
        inner = [_dot(rows(e, js), tb).astype(BF16) for e, tb, js in zip(es, tbs, live)]
        spread = [jnp.concatenate([x[js.index(j) * b:(js.index(j) + 1) * b, :] if j in js else zero
                                   for j in range(n // b)], axis=0) for x, js in zip(inner, live)]
        upd = [_dot(rows(tb, js), x) for tb, x, js in zip(tbs, spread, live)]
        ms = [jnp.concatenate([m[blocks[j], :] - u[js.index(j) * b:(js.index(j) + 1) * b, :] if j in js
                               else m[blocks[j], :] for j in range(n // b)], axis=0)
              for m, u, js in zip(ms, upd, live)]
    return ms


def _chunk_order(n, reverse):
    return range(n - 1, -1, -1) if reverse else range(n)


def _gdn_chunk_problems(d, q_ref, k_ref, v_ref, p_ref, *, c, with_output):
    reverse = d == 1
    t = k_ref.shape[0]
    sp = p_ref[...]
    gc, tot = _chunk_sums(sp, c, reverse)
    strict, incl = _tri_masks(c, reverse)
    problems = []
    for ci in _chunk_order(t // c, reverse):
        rows = slice(ci * c, (ci + 1) * c)
        gcc = gc[rows, :]
        gct = gcc.T
        totc = tot[rows, :]
        for hh in range(GDN_HEADS):
            cb = COL_BETA + d * GDN_HEADS + hh
            cl = COL_GLOGA + d * GDN_HEADS + hh
            hs = slice(hh * GDN_DK, (hh + 1) * GDN_DK)
            kb = k_ref[rows, hs]
            kf = kb.astype(F32)
            vf = v_ref[rows, hs].astype(F32)
            gcol = gcc[:, cl:cl + 1]
            tcol = totc[:, cl:cl + 1]
            beta = sp[rows, cb:cb + 1]
            e_incl = _decay_matrix(gcol, gct[cl:cl + 1, :], incl)
            if with_output:
                qb = q_ref[rows, hs]
                scores = _dot_nt(jnp.concatenate([kb, qb], axis=0), kb)
                kk = scores[:c]
            else:
                kk = _dot_nt(kb, kb)
            pr = {"d": d, "hh": hh, "rows": rows, "hs": hs,
                  "a": kk * jnp.where(strict, e_incl, 0.0) * beta,
                  "rhs": jnp.concatenate([beta * vf, (beta * jnp.exp(gcol)) * kf], axis=1),
                  "k_w": (kf * jnp.exp(tcol - gcol)).astype(BF16),
                  "c_dec": jnp.exp(tcol[0:1, :])}
            if with_output:
                pr["q_w"] = (qb.astype(F32) * jnp.exp(gcol)).astype(BF16)
                pr["p"] = (scores[c:] * e_incl).astype(BF16)
            problems.append(pr)
    return problems


def _gdn_body(*refs, c, with_output):
    s0_ref = refs[0]
    ins = refs[1:9]
    if with_output:
        outs = refs[9:11]
        st_ref = refs[11]
    else:
        outs = (None, None)
        st_ref = refs[9]

    @pl.when(pl.program_id(1) == 0)
    def _():
        st_ref[...] = s0_ref[...]

    per_dir = [_gdn_chunk_problems(d, *ins[4 * d:4 * d + 4], c=c, with_output=with_output) for d in range(2)]
    n_steps = len(per_dir[0]) // GDN_HEADS
    problems = per_dir[0] + per_dir[1]
    minv = _unit_triangular_inverses_minus_eye([pr["a"] for pr in problems], [pr["d"] == 1 for pr in problems],
                                               _merge_level_masks(c))
    for pr, m in zip(problems, minv):
        x = pr["rhs"] + _dot(m.astype(BF16), pr["rhs"].astype(BF16))
        pr["w_v"] = x[:, :GDN_DV]
        pr["w_k"] = x[:, GDN_DV:].astype(BF16)
    state = {(d, hh): st_ref[d, hh] for d in range(2) for hh in range(GDN_HEADS)}
    for step in range(n_steps):
        now = [pr for prs in per_dir for pr in prs[step * GDN_HEADS:(step + 1) * GDN_HEADS]]
        sbs = [state[pr["d"], pr["hh"]].astype(BF16) for pr in now]
        us = [pr["w_v"] - _dot(pr["w_k"], sb) for pr, sb in zip(now, sbs)]
        ubs = [u.astype(BF16) for u in us]
        if with_output:
            for pr, sb, ub in zip(now, sbs, ubs):
                outs[pr["d"]][pr["rows"], pr["hs"]] = _dot(jnp.concatenate([pr["q_w"], pr["p"]], axis=1),
                                                           jnp.concatenate([sb, ub], axis=0))
        for pr, ub in zip(now, ubs):
            key = (pr["d"], pr["hh"])
            state[key] = pr["c_dec"] * state[key] + _dot_tn(pr["k_w"], ub)
    for (d, hh), s in state.items():
        st_ref[d, hh] = s


def _ret_direction(d, q_ref, k_ref, v_ref, lg_ref, o_ref, st_ref, *, c, with_output):
    reverse = d == 1
    t = q_ref.shape[0]
    r = lax.broadcasted_iota(jnp.int32, (c, c), 0)
    col = lax.broadcasted_iota(jnp.int32, (c, c), 1)
    dist = ((col - r) if reverse else (r - col)).astype(F32)
    incl = dist >= 0.0
    pos = lax.broadcasted_iota(jnp.int32, (c, 1), 0).astype(F32)
    steps = (float(c) - pos) if reverse else (pos + 1.0)
    problems, c_decs, q_dec_of = [], [], []
    for hh in range(RET_HEADS):
        hs = slice(hh * RET_DK, (hh + 1) * RET_DK)
        lg = lg_ref[d, hh]
        v_dec = jnp.exp((float(c) - steps) * lg)
        c_decs.append(jnp.exp(jnp.full((1, 1), float(c), F32) * lg))
        if with_output:
            dmat = jnp.where(incl, jnp.exp(jnp.where(incl, dist * lg, 0.0)), 0.0)
            q_dec_of.append(jnp.exp(steps * lg))
        for ci in _chunk_order(t // c, reverse):
            rows = slice(ci * c, (ci + 1) * c)
            kb = k_ref[rows, hs]
            vb = v_ref[rows, hs]
            pr = {"rows": rows, "hs": hs, "inc": _dot_tn(kb, (vb.astype(F32) * v_dec).astype(BF16))}
            if with_output:
                qb = q_ref[rows, hs]
                pr["qb"] = qb
                pr["intra"] = _dot((_dot_nt(qb, kb) * dmat).astype(BF16), vb)
            problems.append(pr)
    n_steps = t // c
    for hh in range(RET_HEADS):
        s = st_ref[d, hh]
        for pr in problems[hh * n_steps:(hh + 1) * n_steps]:
            if with_output:
                o_ref[pr["rows"], pr["hs"]] = pr["intra"] + _dot(pr["qb"], s.astype(BF16)) * q_dec_of[hh]
            s = c_decs[hh] * s + pr["inc"]
        st_ref[d, hh] = s


def _ret_body(*refs, c, with_output):
    lg_ref, s0_ref = refs[0], refs[1]
    ins = refs[2:8]
    if with_output:
        outs = refs[8:10]
        st_ref = refs[10]
    else:
        outs = (None, None)
        st_ref = refs[8]

    @pl.when(pl.program_id(1) == 0)
    def _():
        st_ref[...] = s0_ref[...]

    for d in range(2):
        q_ref, k_ref, v_ref = ins[3 * d:3 * d + 3]
        _ret_direction(d, q_ref, k_ref, v_ref, lg_ref, outs[d], st_ref, c=c, with_output=with_output)


def _ssd_chunk_problems(d, sel_ref, x_ref, b_ref, c_ref, p_ref, o_ref, *, c, with_output):
    reverse = d == 1
    t = x_ref.shape[0]
    sp = p_ref[...]
    gc, tot = _chunk_sums(sp, c, reverse)
    col = lax.broadcasted_iota(jnp.int32, sp.shape, 1)
    lo = COL_SLOGA + d * SSD_HEADS
    mine = jnp.where(col >= lo, jnp.where(col < lo + SSD_HEADS, 1.0, 0.0), 0.0)
    gcm = gc * mine
    totm = tot * mine
    sel_d = sel_ref[d, 0]
    sel_l = sel_ref[d, 1]
    xd = x_ref[...].astype(F32) * _dot(sp.astype(BF16), sel_d)
    xdb = xd.astype(BF16)
    n_chunks = t // c
    stack = [jnp.exp(totm - gcm).astype(BF16)]
    if with_output:
        stack.append(jnp.exp(gcm).astype(BF16))
    tot_base = len(stack) * t
    for ci in range(n_chunks):
        stack += list(_split3(jnp.exp(totm[ci * c:ci * c + BF16_ROWS, :])))
    expanded = _dot(jnp.concatenate(stack, axis=0), sel_l)
    vw = (xd * expanded[0:t, :]).astype(BF16)
    if with_output:
        e1x = expanded[t:2 * t, :]
        _, incl = _tri_masks(c, reverse)
        lane = lax.broadcasted_iota(jnp.int32, (c, 2 * SSD_HEADDIM), 1)
        first_head = lane < SSD_HEADDIM
    gw = SSD_HG * SSD_HEADDIM
    problems = []
    for ci in _chunk_order(t // c, reverse):
        rows = slice(ci * c, (ci + 1) * c)
        gcc = gc[rows, :]
        gct = gcc.T
        base = tot_base + 3 * BF16_ROWS * ci
        c_dec = (expanded[base:base + 1, :] + expanded[base + BF16_ROWS:base + BF16_ROWS + 1, :]
                 + expanded[base + 2 * BF16_ROWS:base + 2 * BF16_ROWS + 1, :])
        for gg in range(SSD_GROUPS):
            gs = slice(gg * SSD_STATE, (gg + 1) * SSD_STATE)
            gcols = slice(gg * gw, (gg + 1) * gw)
            bb = b_ref[rows, gs]
            pr = {"d": d, "gg": gg, "rows": rows, "gcols": gcols,
                  "inc": _dot_tn(bb, vw[rows, gcols]), "c_dec": c_dec[:, gcols]}
            if with_output:
                cb = c_ref[rows, gs]
                scores = _dot_nt(cb, bb)
                for pair in range(SSD_HG // 2):
                    h0 = gg * SSD_HG + 2 * pair
                    cols = slice(h0 * SSD_HEADDIM, (h0 + 2) * SSD_HEADDIM)
                    ms = []
                    for hd in (h0, h0 + 1):
                        cl = lo + hd
                        dmat = _decay_matrix(gcc[:, cl:cl + 1], gct[cl:cl + 1, :], incl)
                        ms.append((scores * dmat).astype(BF16))
                    xp = xdb[rows, cols]
                    zero = jnp.zeros_like(xp)
                    rhs = jnp.concatenate([jnp.where(first_head, xp, zero), jnp.where(first_head, zero, xp)], axis=0)
                    o_ref[rows, cols] = _dot(jnp.concatenate(ms, axis=1), rhs)
                pr["cb"] = cb
                pr["q_dec"] = e1x[rows, gcols]
            problems.append(pr)
    return problems


def _ssd_body(*refs, c, with_output):
    sel_ref, s0_ref = refs[0], refs[1]
    ins = refs[2:10]
    if with_output:
        outs = refs[10:12]
        st_ref = refs[12]
    else:
        outs = (None, None)
        st_ref = refs[10]

    @pl.when(pl.program_id(1) == 0)
    def _():
        st_ref[...] = s0_ref[...]

    per_dir = [_ssd_chunk_problems(d, sel_ref, *ins[4 * d:4 * d + 4], outs[d], c=c, with_output=with_output)
               for d in range(2)]
    n_steps = len(per_dir[0]) // SSD_GROUPS
    state = {(d, gg): st_ref[d, gg] for d in range(2) for gg in range(SSD_GROUPS)}
    for step in range(n_steps):
        for prs in per_dir:
            for pr in prs[step * SSD_GROUPS:(step + 1) * SSD_GROUPS]:
                key = (pr["d"], pr["gg"])
                if with_output:
                    o_ref = outs[pr["d"]]
                    inter = _dot(pr["cb"], state[key].astype(BF16))
                    o_ref[pr["rows"], pr["gcols"]] = o_ref[pr["rows"], pr["gcols"]] + inter * pr["q_dec"]
                state[key] = pr["c_dec"] * state[key] + pr["inc"]
    for (d, gg), s in state.items():
        st_ref[d, gg] = s


def ssd_head_selectors():
    sel = np.zeros((2, 2, SMALL_W, SSD_DINNER), np.float32)
    heads = np.arange(SSD_DINNER) // SSD_HEADDIM
    for d in range(2):
        sel[d, 0, COL_DELTA + d * SSD_HEADS + heads, np.arange(SSD_DINNER)] = 1.0
        sel[d, 1, COL_SLOGA + d * SSD_HEADS + heads, np.arange(SSD_DINNER)] = 1.0
    return jnp.asarray(sel, BF16)


def _scan_call(body, name, s0, arrays, out_width, with_output, extra_args=(), extra_specs=(), block=SCAN_BLOCK):
    b_, seq, _ = arrays[0].shape
    t = min(block, seq)
    c = min(SCAN_CHUNK, t)
    nt = seq // t
    fwd = lambda w: pl.BlockSpec((None, t, w), lambda b, i: (b, i, 0))
    bwd = lambda w: pl.BlockSpec((None, t, w), lambda b, i: (b, nt - 1 - i, 0))
    st_spec = pl.BlockSpec((None,) + s0.shape[1:], lambda b, i: (b,) + (0,) * (s0.ndim - 1))
    in_specs = list(extra_specs) + [st_spec]
    in_specs += [fwd(a.shape[-1]) for a in arrays] + [bwd(a.shape[-1]) for a in arrays]
    out_specs, out_shape = [], []
    if with_output:
        out_specs += [fwd(out_width), bwd(out_width)]
        out_shape += [jax.ShapeDtypeStruct((b_, seq, out_width), F32)] * 2
    out_specs.append(st_spec)
    out_shape.append(jax.ShapeDtypeStruct(s0.shape, F32))
    res = pl.pallas_call(
        functools.partial(body, c=c, with_output=with_output),
        grid=(b_, nt),
        in_specs=in_specs,
        out_specs=out_specs,
        out_shape=out_shape,
        compiler_params=_cparams(2),
        name=name,
    )(*extra_args, s0, *arrays, *arrays)
    if with_output:
        return res[0], res[1], res[2]
    return None, None, res[0]


def token_mixers(proj, prep, states, lp, with_output):
    gdn_s0, ret_s0, ssd_s0 = states
    a_f, a_b, gdn_s = _scan_call(_gdn_body, "gdn_scan", gdn_s0,
                                 [prep["gq"], prep["gk"], prep["gv"], prep["smallp"]],
                                 GDN_HEADS * GDN_DV, with_output)
    ret_lg = -jnp.exp(lp["ret_decay"].astype(F32))
    sel = ssd_head_selectors()
    b_f, b_b, ret_s = _scan_call(_ret_body, "ret_scan", ret_s0,
                                 [prep["rq"], prep["rk"], proj["r_v"]],
                                 RET_HEADS * RET_DV, with_output, block=RET_SCAN_BLOCK, extra_args=(ret_lg,),
                                 extra_specs=(pl.BlockSpec(memory_space=pltpu.SMEM),))
    c_f, c_b, ssd_s = _scan_call(_ssd_body, "ssd_scan", ssd_s0,
                                 [prep["sx"], prep["sb"], prep["sc"], prep["smallp"]],
                                 SSD_DINNER, with_output, extra_args=(sel,),
                                 extra_specs=(_resident(sel.shape),))
    o = None
    if with_output:
        o = {"a_f": a_f, "a_b": a_b, "b_f": b_f, "b_b": b_b, "c_f": c_f, "c_b": c_b}
    return o, (gdn_s, ret_s, ssd_s)


def zero_states(b_):
    return (jnp.zeros((b_, 2, GDN_HEADS, GDN_DK, GDN_DV), F32),
            jnp.zeros((b_, 2, RET_HEADS, RET_DK, RET_DV), F32),
            jnp.zeros((b_, 2, SSD_GROUPS, SSD_STATE, SSD_HG * SSD_HEADDIM), F32))


def _merge_body(h_ref, mod_ref, oaf_ref, oab_ref, obf_ref, obb_ref, ocf_ref, ocb_ref, sx_ref,
                gz_ref, rg_ref, sz_ref, brg_ref, gng_ref, rng_ref, sd_ref, sng_ref,
                wa_ref, wb_ref, wc_ref, wo_ref, lng_ref, lnb_ref, o_ref, *, alpha):
    d = h_ref.shape[-1]
    oa = oaf_ref[...] + oab_ref[...]
    gz = gz_ref[...].astype(F32)
    ya = []
    for hh in range(GDN_HEADS):
        x = oa[:, hh * GDN_DV:(hh + 1) * GDN_DV]
        ms = jnp.mean(x * x, axis=-1, keepdims=True)
        ya.append(x * lax.rsqrt(ms + RMS_EPS) * gng_ref[...] * gz[:, hh * GDN_DV:(hh + 1) * GDN_DV])
    ya = jnp.concatenate(ya, axis=-1).astype(BF16)
    ob = obf_ref[...] + obb_ref[...]
    rg = rg_ref[...].astype(F32)
    yb = []
    for hh in range(RET_HEADS):
        sl = slice(hh * RET_DV, (hh + 1) * RET_DV)
        x = ob[:, sl]
        mu = jnp.mean(x, axis=-1, keepdims=True)
        xc = x - mu
        var = jnp.mean(xc * xc, axis=-1, keepdims=True)
        yb.append(xc * lax.rsqrt(var + LN_EPS) * rng_ref[:, sl] * rg[:, sl])
    yb = jnp.concatenate(yb, axis=-1).astype(BF16)
    oc = (ocf_ref[...] + ocb_ref[...] + sd_ref[...] * sx_ref[...].astype(F32)) * sz_ref[...].astype(F32)
    gw = SSD_DINNER // SSD_GROUPS
    yc = []
    for gg in range(SSD_GROUPS):
        sl = slice(gg * gw, (gg + 1) * gw)
        x = oc[:, sl]
        ms = jnp.mean(x * x, axis=-1, keepdims=True)
        yc.append(x * lax.rsqrt(ms + RMS_EPS) * sng_ref[:, sl])
    yc = jnp.concatenate(yc, axis=-1).astype(BF16)
    gates = brg_ref[...].astype(F32)
    merged = (gates[:, 0:d] * _dot(ya, wa_ref[...])
              + gates[:, d:2 * d] * _dot(yb, wb_ref[...])
              + gates[:, 2 * d:3 * d] * _dot(yc, wc_ref[...]))
    mix = _dot(merged.astype(BF16), wo_ref[...])
    h = h_ref[...]
    y = alpha * h + mod_ref[5:6, :] * mix
    o_ref[...] = _layer_norm_rows(y, lng_ref[...], lnb_ref[...])


def mixer_merge(h, mod, o, proj, sx, lp, ln_g, ln_b, *, alpha, tm):
    b_, seq, d = h.shape
    tm = min(tm, seq)

    def tok(w):
        return pl.BlockSpec((None, tm, w), lambda b, i: (b, i, 0))

    ssd_d_cols = jnp.repeat(lp["ssd_d"].astype(F32), SSD_HEADDIM).reshape(1, SSD_DINNER)
    args = [h, mod, o["a_f"], o["a_b"], o["b_f"], o["b_b"], o["c_f"], o["c_b"], sx,
            proj["g_z"], proj["r_g"], proj["s_z"], proj["br_gate"],
            lp["gdn_norm_g"].astype(F32).reshape(1, GDN_DV),
            lp["ret_norm_g"].astype(F32).reshape(1, RET_HEADS * RET_DV),
            ssd_d_cols, lp["ssd_norm_g"].astype(F32).reshape(1, SSD_DINNER),
            lp["w_br_a"], lp["w_br_b"], lp["w_br_c"], lp["w_out"], ln_g.reshape(1, d), ln_b.reshape(1, d)]
    in_specs = [tok(d), pl.BlockSpec((None, N_ADA, d), lambda b, i: (b, 0, 0)),
                tok(GDN_HEADS * GDN_DV), tok(GDN_HEADS * GDN_DV),
                tok(RET_HEADS * RET_DV), tok(RET_HEADS * RET_DV),
                tok(SSD_DINNER), tok(SSD_DINNER), tok(SSD_DINNER),
                tok(GDN_HEADS * GDN_DV), tok(RET_HEADS * RET_DV), tok(SSD_DINNER), tok(N_BRANCH * d)]
    lead = (lp["layer"],)
    in_specs += [_resident_slice(a, lead) if a.ndim == 3 else _resident(a.shape) for a in args[13:]]
    return pl.pallas_call(
        functools.partial(_merge_body, alpha=alpha),
        grid=(b_, seq // tm),
        in_specs=in_specs,
        out_specs=tok(d),
        out_shape=jax.ShapeDtypeStruct(h.shape, F32),
        compiler_params=_cparams(2),
        name="mixer_merge",
    )(*args)


def kernel(x, c, ctx, c_ctx, ada_w, ada_b, ln_g, ln_b, ffn_w13, ffn_w2, mix_w_in,
           gdn_conv_w, gdn_a_log, gdn_dt_bias, gdn_norm_g, ret_decay, ret_norm_g,
           ssd_conv_w, ssd_conv_b, ssd_a_log, ssd_dt_bias, ssd_d, ssd_norm_g,
           w_br_a, w_br_b, w_br_c, mix_w_out):
    depth = ada_w.shape[0]
    alpha = float((2 * depth) ** 0.25)
    b_, seq, d = x.shape
    rope2 = rope_tables(seq)
    c_rows = jnp.concatenate([c, c_ctx[None, :]], axis=0)
    h, hc = x, ctx
    w13, w2 = ffn_w13.astype(BF16), ffn_w2.astype(BF16)
    w_in_r = split_w_in(mix_w_in, d)
    w_branch = {'w_br_a': w_br_a.astype(BF16), 'w_br_b': w_br_b.astype(BF16),
                'w_br_c': w_br_c.astype(BF16), 'w_out': mix_w_out.astype(BF16)}
    ada_b3 = ada_b.reshape(depth, 1, -1)
    for i in range(depth):
        last = i == depth - 1
        mod_all = ada_modulation(c_rows, ada_w, ada_b3, i)
        mod = mod_all[:b_].reshape(b_, N_ADA, d)
        mod_c = jnp.broadcast_to(mod_all[b_:].reshape(1, N_ADA, d), (b_, N_ADA, d))
        lp = {'layer': i, 'gdn_conv_w': gdn_conv_w[i], 'gdn_a_log': gdn_a_log[i],
              'gdn_dt_bias': gdn_dt_bias[i], 'gdn_norm_g': gdn_norm_g[i], 'ret_decay': ret_decay[i],
              'ret_norm_g': ret_norm_g[i], 'ssd_conv_w': ssd_conv_w[i], 'ssd_conv_b': ssd_conv_b[i],
              'ssd_a_log': ssd_a_log[i], 'ssd_dt_bias': ssd_dt_bias[i], 'ssd_d': ssd_d[i],
              'ssd_norm_g': ssd_norm_g[i], **w_branch}
        ffn = functools.partial(ffn_sublayer, alpha=alpha, tm=1024)
        h = ffn(h, mod, w13, w2, (i, 0), ln_g[i, 0], ln_b[i, 0], mod_base=0)
        hc = ffn(hc, mod_c, w13, w2, (i, 0), ln_g[i, 0], ln_b[i, 0], mod_base=0)
        proj_c, prep_c = mixer_in_projection_prep(hc, mod_c, w_in_r, None, lp, tm=256)
        o_c, ctx_states = token_mixers(proj_c, prep_c, zero_states(b_), lp, not last)
        proj, prep = mixer_in_projection_prep(h, mod, w_in_r, rope2, lp, tm=512)
        o_l, _ = token_mixers(proj, prep, ctx_states, lp, True)
        h = mixer_merge(h, mod, o_l, proj, prep["sx"], lp, ln_g[i, 1], ln_b[i, 1], alpha=alpha, tm=256)
        h = ffn(h, mod, w13, w2, (i, 1), ln_g[i, 2], ln_b[i, 2], mod_base=6)
        if not last:
            hc = mixer_merge(hc, mod_c, o_c, proj_c, prep_c["sx"], lp, ln_g[i, 1], ln_b[i, 1],
                             alpha=alpha, tm=256)
            hc = ffn(hc, mod_c, w13, w2, (i, 1), ln_g[i, 2], ln_b[i, 2], mod_base=6)
    return h
```

```python
import functools

import numpy as np
import jax
import jax.numpy as jnp
from jax import lax
from jax.experimental import pallas as pl
from jax.experimental.pallas import tpu as pltpu

F32 = jnp.float32
BF16 = jnp.bfloat16

GRID_W = 64
CONV_K = 5
GDN_HEADS = 4
GDN_DK = 128
GDN_DV = 128
RET_HEADS = 4
RET_DK = 128
RET_DV = 128
SSD_HEADS = 16
SSD_HEADDIM = 64
SSD_GROUPS = 2
SSD_STATE = 128
SSD_DINNER = SSD_HEADS * SSD_HEADDIM
SSD_HG = SSD_HEADS // SSD_GROUPS
N_BRANCH = 3
N_ADA = 9
ROPE_BASE = 10000.0
GDN_QKV = GDN_HEADS * (2 * GDN_DK + GDN_DV)
SSD_XBC = SSD_DINNER + 2 * SSD_GROUPS * SSD_STATE
LN_EPS = 1e-5
RMS_EPS = 1e-6
L2_EPS = 1e-6

VMEM_LIMIT_BYTES = 56 * 1024 * 1024
LANE = 128
SMALL_W = LANE
BF16_ROWS = 16
CONV_ROWS = 128
FFN_ROWS = 256
FFN_FF_CHUNK = 1024
FFN_TILE = 1024
INPROJ_TILE = 512
MERGE_TILE = 256
ADA_COLS = 1024
MOD_FFN1, MOD_MIX, MOD_FFN2 = 0, 3, 6
INPROJ_ROWS = 256
SCAN_CHUNK = 128
SCAN_BLOCK = 256
RET_SCAN_BLOCK = 512

COL_BETA = 0
COL_GLOGA = 2 * GDN_HEADS
COL_DELTA = 4 * GDN_HEADS
COL_SLOGA = COL_DELTA + 2 * SSD_HEADS
COL_END = COL_SLOGA + 2 * SSD_HEADS


def _cparams(n_axes):
    return pltpu.CompilerParams(dimension_semantics=("arbitrary",) * n_axes,
                                vmem_limit_bytes=VMEM_LIMIT_BYTES)


def _resident(shape):
    nd = len(shape)
    return pl.BlockSpec(shape, lambda *_: (0,) * nd, pipeline_mode=pl.Buffered(1))


def _resident_slice(arr, lead):
    tail = tuple(arr.shape[len(lead):])
    return pl.BlockSpec((None,) * len(lead) + tail, lambda *_: tuple(lead) + (0,) * len(tail),
                        pipeline_mode=pl.Buffered(1))


def _layer_norm_rows(y, g, b):
    mu = jnp.mean(y, axis=-1, keepdims=True)
    yc = y - mu
    var = jnp.mean(yc * yc, axis=-1, keepdims=True)
    return yc * lax.rsqrt(var + LN_EPS) * g + b


def _silu(x):
    return x * jax.nn.sigmoid(x)


def _softplus(x):
    return jnp.maximum(x, 0.0) + jnp.log1p(jnp.exp(-jnp.abs(x)))


def _dot(a, b):
    return jnp.dot(a, b, preferred_element_type=F32)


def _dot_nt(a, b):
    return lax.dot_general(a, b, (((1,), (1,)), ((), ())), preferred_element_type=F32)


def _dot_tn(a, b):
    return lax.dot_general(a, b, (((0,), (0,)), ((), ())), preferred_element_type=F32)


def _ada_body(c_ref, w_ref, b_ref, o_ref):
    s = _silu(c_ref[...]).astype(BF16)
    o_ref[...] = _dot(s, w_ref[...].astype(BF16)) + b_ref[...]


def ada_modulation(c_rows, w, b, layer):
    r, d = c_rows.shape
    n = w.shape[-1]
    tn = ADA_COLS
    return pl.pallas_call(
        _ada_body,
        grid=(n // tn,),
        in_specs=[pl.BlockSpec((r, d), lambda j: (0, 0)),
                  pl.BlockSpec((None, d, tn), lambda j: (layer, 0, j)),
                  pl.BlockSpec((None, 1, tn), lambda j: (layer, 0, j))],
        out_specs=pl.BlockSpec((r, tn), lambda j: (0, j)),
        out_shape=jax.ShapeDtypeStruct((r, n), F32),
        compiler_params=_cparams(1),
        name="ada_modulation",
    )(c_rows, w, b)


def _ffn_body(h_ref, mod_ref, w13_ref, w2_ref, lng_ref, lnb_ref, o_ref, *, alpha, ff, chunks, mod_base):
    shift = mod_ref[mod_base:mod_base + 1, :]
    scale = mod_ref[mod_base + 1:mod_base + 2, :]
    gate = mod_ref[mod_base + 2:mod_base + 3, :]
    tm = h_ref.shape[0]
    rb = min(tm, FFN_ROWS)
    for r0 in range(0, tm, rb):
        h = h_ref[r0:r0 + rb, :]
        u = (h * (1.0 + scale) + shift).astype(BF16)
        acc = None
        for c0, c1 in chunks:
            a = _dot(u, w13_ref[:, c0:c1])
            b = _dot(u, w13_ref[:, ff + c0:ff + c1])
            g = (_silu(a) * b).astype(BF16)
            p = _dot(g, w2_ref[c0:c1, :])
            acc = p if acc is None else acc + p
        y = alpha * h + (0.5 * gate) * acc
        o_ref[r0:r0 + rb, :] = _layer_norm_rows(y, lng_ref[...], lnb_ref[...])


def _ff_chunks(ff, width=FFN_FF_CHUNK):
    return tuple((c0, min(c0 + width, ff)) for c0 in range(0, ff, width))


def ffn_sublayer(h, mod, w13, w2, which, ln_g, ln_b, *, alpha, mod_base, tm):
    b_, seq, d = h.shape
    ff = w2.shape[-2]
    tm = min(tm, seq)
    body = functools.partial(_ffn_body, alpha=alpha, ff=ff, chunks=_ff_chunks(ff), mod_base=mod_base)
    return pl.pallas_call(
        body,
        grid=(b_, seq // tm),
        in_specs=[pl.BlockSpec((None, tm, d), lambda b, i: (b, i, 0)),
                  pl.BlockSpec((None, N_ADA, d), lambda b, i: (b, 0, 0)),
                  _resident_slice(w13, which), _resident_slice(w2, which),
                  _resident((1, d)), _resident((1, d))],
        out_specs=pl.BlockSpec((None, tm, d), lambda b, i: (b, i, 0)),
        out_shape=jax.ShapeDtypeStruct(h.shape, F32),
        compiler_params=_cparams(2),
        name="ffn_sublayer",
    )(h, mod, w13, w2, ln_g.reshape(1, d), ln_b.reshape(1, d))


W_IN_PIECES = ((("g_qkv", GDN_QKV), ("g_z", GDN_HEADS * GDN_DV)),
               (("r_q", RET_HEADS * RET_DK), ("r_k", RET_HEADS * RET_DK), ("r_v", RET_HEADS * RET_DV),
                ("r_g", RET_HEADS * RET_DV), ("s_z", SSD_DINNER), ("s_xbc", SSD_XBC)),
               (("br_gate", None),),
               (("small", SMALL_W),))


def split_w_in(w_in, d):
    sizes = (GDN_QKV, GDN_HEADS * GDN_DV, 2 * GDN_HEADS, 2 * GDN_HEADS,
             RET_HEADS * RET_DK, RET_HEADS * RET_DK, RET_HEADS * RET_DV, RET_HEADS * RET_DV,
             SSD_DINNER, SSD_XBC, 2 * SSD_HEADS, N_BRANCH * d)
    offs = np.concatenate([[0], np.cumsum(sizes)])
    w_in = w_in.astype(BF16)
    n_small = 4 * GDN_HEADS + 2 * SSD_HEADS
    pad = jnp.zeros(w_in.shape[:-1] + (SMALL_W - n_small,), w_in.dtype)
    small = jnp.concatenate([w_in[..., offs[2]:offs[4]], w_in[..., offs[10]:offs[11]], pad], axis=-1)
    return (w_in[..., offs[0]:offs[2]], w_in[..., offs[4]:offs[10]], w_in[..., offs[11]:offs[12]], small)


F32_ROWS = 8
def _w_in_offsets(d):
    offs = {}
    for piece, groups in enumerate(W_IN_PIECES):
        c0 = 0
        for name, width in groups:
            width = N_BRANCH * d if width is None else width
            offs[name] = (piece, c0, c0 + width)
            c0 += width
    return offs


def _conv_silu_rows(raw, tm, w_ref, b_ref, first, last):
    h8 = F32_ROWS
    xx = jnp.concatenate([jnp.where(first, 0.0, raw[tm:tm + h8, :]), raw[0:tm, :],
                          jnp.where(last, 0.0, raw[tm + h8:tm + 2 * h8, :])], axis=0).astype(BF16)
    rb = min(tm, CONV_ROWS)
    win = rb + 2 * h8
    r = lax.broadcasted_iota(jnp.int32, (rb, CONV_K * win), 0)
    col = lax.broadcasted_iota(jnp.int32, (rb, CONV_K * win), 1)
    tap = jnp.zeros_like(col)
    for k in range(1, CONV_K):
        tap = tap + jnp.where(col >= k * win, 1, 0)
    shifts = jnp.where(col == r + tap * (win + 1) + (h8 - CONV_K // 2), 1.0, 0.0).astype(BF16)
    wb = [w_ref[k:k + 1, :].astype(BF16) for k in range(CONV_K)]
    out = []
    for blk in range(tm // rb):
        window = xx[blk * rb:blk * rb + win, :]
        out.append(_dot(shifts, jnp.concatenate([window * wk for wk in wb], axis=0)))
    acc = out[0] if len(out) == 1 else jnp.concatenate(out, axis=0)
    if b_ref is not None:
        acc = acc + b_ref[...]
    return _silu(acc)


def _inproj_prep_body(*refs, use_rope, offs):
    hp_ref, h_ref, hn_ref, mod_ref = refs[:4]
    w_refs = refs[4:4 + len(W_IN_PIECES)]
    k = 4 + len(W_IN_PIECES)
    if use_rope:
        cos_ref, sin_ref = refs[k:k + 2]
        k += 2
    gw_ref, sw_ref, sb_ref, bias_ref, alog_ref = refs[k:k + 5]
    (gq_o, gk_o, gv_o, gz_o, rq_o, rk_o, rv_o, rg_o, sz_o, sx_o, sb_o, sc_o, brg_o, sm_o) = refs[k + 5:]
    i = pl.program_id(1)
    tile_first = i == 0
    tile_last = i == pl.num_programs(1) - 1
    tile = h_ref.shape[0]
    tm = min(tile, INPROJ_ROWS)
    h8 = F32_ROWS
    shift = mod_ref[MOD_MIX:MOD_MIX + 1, :]
    scale = mod_ref[MOD_MIX + 1:MOD_MIX + 2, :]

    def proj(lhs, name):
        piece, c0, c1 = offs[name]
        return _dot(lhs, w_refs[piece][:, c0:c1])

    for r0 in range(0, tile, tm):
        rows = slice(r0, r0 + tm)
        before = hp_ref[...] if r0 == 0 else h_ref[r0 - h8:r0, :]
        after = hn_ref[...] if r0 + tm == tile else h_ref[r0 + tm:r0 + tm + h8, :]
        first = tile_first if r0 == 0 else False
        last = tile_last if r0 + tm == tile else False
        u_ext = (jnp.concatenate([h_ref[rows, :], before, after], axis=0) * (1.0 + scale) + shift).astype(BF16)
        u = u_ext[0:tm, :]
        raw_g = proj(u_ext, "g_qkv")
        raw_s = proj(u_ext, "s_xbc")

        qkv = _conv_silu_rows(raw_g, tm, gw_ref, None, first, last)
        rq = proj(u, "r_q")
        rk = proj(u, "r_k") * (RET_DK ** -0.5)
        nq = GDN_HEADS * GDN_DK
        for hh in range(GDN_HEADS):
            sl = slice(hh * GDN_DK, (hh + 1) * GDN_DK)
            q = qkv[:, sl]
            kk = qkv[:, nq + hh * GDN_DK: nq + (hh + 1) * GDN_DK]
            q = q * lax.rsqrt(jnp.sum(q * q, axis=-1, keepdims=True) + L2_EPS) * (GDN_DK ** -0.5)
            kk = kk * lax.rsqrt(jnp.sum(kk * kk, axis=-1, keepdims=True) + L2_EPS)
            gq_o[rows, sl] = q.astype(BF16)
            gk_o[rows, sl] = kk.astype(BF16)
        gv_o[rows, :] = qkv[:, 2 * nq:].astype(BF16)

        xbc = _conv_silu_rows(raw_s, tm, sw_ref, sb_ref, first, last)
        sm = proj(u, "small")
        rv_o[rows, :] = proj(u, "r_v").astype(BF16)
        gz_o[rows, :] = _silu(proj(u, "g_z")).astype(BF16)
        sx_o[rows, :] = xbc[:, :SSD_DINNER].astype(BF16)
        sb_o[rows, :] = xbc[:, SSD_DINNER:SSD_DINNER + SSD_GROUPS * SSD_STATE].astype(BF16)
        sc_o[rows, :] = xbc[:, SSD_DINNER + SSD_GROUPS * SSD_STATE:].astype(BF16)

        rg_o[rows, :] = _silu(proj(u, "r_g")).astype(BF16)
        sz_o[rows, :] = _silu(proj(u, "s_z")).astype(BF16)
        if use_rope:
            cos2 = cos_ref[rows, :]
            sin2 = sin_ref[rows, :]
            for hh in range(RET_HEADS):
                sl = slice(hh * RET_DK, (hh + 1) * RET_DK)
                for src, dst in ((rq, rq_o), (rk, rk_o)):
                    xh = src[:, sl]
                    dst[rows, sl] = (xh * cos2 + pltpu.roll(xh, RET_DK // 2, axis=1) * sin2).astype(BF16)
        else:
            rq_o[rows, :] = rq.astype(BF16)
            rk_o[rows, :] = rk.astype(BF16)

        col = lax.broadcasted_iota(jnp.int32, sm.shape, 1)
        shifted = pltpu.roll(sm, COL_SLOGA - COL_DELTA, axis=1)
        xin = jnp.where(col < COL_SLOGA, sm, shifted)
        sp = _softplus(xin + bias_ref[...])
        neg_a = -jnp.exp(alog_ref[...])
        sm_o[rows, :] = jnp.where(col < COL_GLOGA, jax.nn.sigmoid(sm),
                                  jnp.where(col < COL_DELTA, neg_a * sp,
                                            jnp.where(col < COL_SLOGA, sp,
                                                      jnp.where(col < COL_END, sp * neg_a, 0.0))))
        brg_o[rows, :] = jax.nn.sigmoid(proj(u, "br_gate")).astype(BF16)


def mixer_in_projection_prep(h, mod, w_in_r, rope2, lp, *, tm):
    b_, seq, d = h.shape
    tm = min(tm, seq)
    nh = tm // F32_ROWS
    n_halo = seq // F32_ROWS
    use_rope = rope2 is not None
    offs = _w_in_offsets(d)

    def tok(w):
        return pl.BlockSpec((None, tm, w), lambda b, i: (b, i, 0))

    zeros = lambda n: jnp.zeros((n,), F32)
    bias_row = jnp.concatenate([zeros(COL_GLOGA), lp["gdn_dt_bias"].astype(F32).reshape(-1),
                                lp["ssd_dt_bias"].astype(F32).reshape(-1),
                                lp["ssd_dt_bias"].astype(F32).reshape(-1),
                                zeros(SMALL_W - COL_END)]).reshape(1, SMALL_W)
    alog_row = jnp.concatenate([zeros(COL_GLOGA), lp["gdn_a_log"].astype(F32).reshape(-1),
                                zeros(COL_SLOGA - COL_DELTA), lp["ssd_a_log"].astype(F32).reshape(-1),
                                zeros(SMALL_W - COL_END)]).reshape(1, SMALL_W)
    args = [h, h, h, mod, *w_in_r]
    in_specs = [pl.BlockSpec((None, F32_ROWS, d), lambda b, i: (b, jnp.maximum(i * nh - 1, 0), 0)),
                tok(d),
                pl.BlockSpec((None, F32_ROWS, d), lambda b, i: (b, jnp.minimum((i + 1) * nh, n_halo - 1), 0)),
                pl.BlockSpec((None, N_ADA, d), lambda b, i: (b, 0, 0)),
                *[_resident_slice(w, (lp["layer"],)) for w in w_in_r]]
    if use_rope:
        args += [rope2[0], rope2[1]]
        in_specs += [pl.BlockSpec((tm, RET_DK), lambda b, i: (i, 0))] * 2
    consts = [lp["gdn_conv_w"].astype(F32), lp["ssd_conv_w"].astype(F32),
              lp["ssd_conv_b"].astype(F32).reshape(1, SSD_XBC), bias_row, alog_row]
    args += consts
    in_specs += [_resident(a.shape) for a in consts]
    names = ("gq", "gk", "gv", "g_z", "rq", "rk", "r_v", "r_g", "s_z", "sx", "sb", "sc", "br_gate", "smallp")
    hw = GDN_HEADS * GDN_DK
    widths = (hw, hw, GDN_HEADS * GDN_DV, GDN_HEADS * GDN_DV, RET_HEADS * RET_DK, RET_HEADS * RET_DK,
              RET_HEADS * RET_DV, RET_HEADS * RET_DV, SSD_DINNER, SSD_DINNER, SSD_GROUPS * SSD_STATE,
              SSD_GROUPS * SSD_STATE, N_BRANCH * d, SMALL_W)
    dtypes = (BF16,) * 13 + (F32,)
    outs = pl.pallas_call(
        functools.partial(_inproj_prep_body, use_rope=use_rope, offs=offs),
        grid=(b_, seq // tm),
        in_specs=in_specs,
        out_specs=[tok(w) for w in widths],
        out_shape=[jax.ShapeDtypeStruct((b_, seq, w), t) for w, t in zip(widths, dtypes)],
        compiler_params=_cparams(2),
        name="mixer_in_projection_prep",
    )(*args)
    res = dict(zip(names, outs))
    gates = {n: res[n] for n in ("g_z", "r_v", "r_g", "s_z", "br_gate")}
    prep = {n: res[n] for n in ("gq", "gk", "gv", "rq", "rk", "sx", "sb", "sc", "smallp")}
    return gates, prep


def rope_tables(seq_len):
    rows = seq_len // GRID_W
    row_id = jnp.repeat(jnp.arange(rows, dtype=F32), GRID_W)
    col_id = jnp.tile(jnp.arange(GRID_W, dtype=F32), rows)
    n_freq = RET_DK // 4
    inv_freq = ROPE_BASE ** (-jnp.arange(n_freq, dtype=F32) / n_freq)
    ang = jnp.concatenate([row_id[:, None] * inv_freq, col_id[:, None] * inv_freq], axis=-1)
    cos, sin = jnp.cos(ang), jnp.sin(ang)
    return jnp.concatenate([cos, cos], axis=-1), jnp.concatenate([-sin, sin], axis=-1)


def _tri_masks(c, reverse):
    r = lax.broadcasted_iota(jnp.int32, (c, c), 0)
    col = lax.broadcasted_iota(jnp.int32, (c, c), 1)
    if reverse:
        return col > r, col >= r
    return col < r, col <= r


def _split3(x):
    hi = x.astype(BF16)
    r1 = x - hi.astype(F32)
    mid = r1.astype(BF16)
    lo = (r1 - mid.astype(F32)).astype(BF16)
    return hi, mid, lo


def _chunk_sums(x, c, reverse):
    t, w = x.shape
    r = lax.broadcasted_iota(jnp.int32, (c, c), 0)
    col = lax.broadcasted_iota(jnp.int32, (c, c), 1)
    m_cum = jnp.where((col >= r) if reverse else (col <= r), 1.0, 0.0).astype(BF16)
    cums, tots = [], []
    for ci in range(t // c):
        three = _dot(m_cum, jnp.concatenate(_split3(x[ci * c:(ci + 1) * c, :]), axis=1))
        cum = three[:, 0:w] + three[:, w:2 * w] + three[:, 2 * w:3 * w]
        end = 0 if reverse else c - 1
        cums.append(cum)
        tots.append(jnp.broadcast_to(cum[end:end + 1, :], (c, w)))
    if len(cums) == 1:
        return cums[0], tots[0]
    return jnp.concatenate(cums, axis=0), jnp.concatenate(tots, axis=0)


def _decay_matrix(gcol, grow, incl):
    return jnp.where(incl, jnp.exp(jnp.where(incl, gcol - grow, 0.0)), 0.0)


def _merge_level_masks(c):
    r = lax.broadcasted_iota(jnp.int32, (c, c), 0)
    col = lax.broadcasted_iota(jnp.int32, (c, c), 1)
    x = lax.bitwise_xor(r, col)
    return [lax.shift_right_logical(x, k) == 1 for k in range(int(np.log2(c)))]


def _unit_triangular_inverses_minus_eye(a_mats, upper, levels):
    shape = a_mats[0].shape
    eye = jnp.where(lax.broadcasted_iota(jnp.int32, shape, 0) == lax.broadcasted_iota(jnp.int32, shape, 1),
                    1.0, 0.0)
    n = shape[0]
    ms = [eye - jnp.where(levels[0], a, 0.0) for a in a_mats]
    for k, lvl in enumerate(levels[1:], start=1):
        b = 1 << k
        tbs = [m.astype(BF16) for m in ms]
        es = [jnp.where(lvl, a, 0.0).astype(BF16) for a in a_mats]
        if b % BF16_ROWS:
            inner = [_dot(e, tb).astype(BF16) for e, tb in zip(es, tbs)]
            ms = [m - _dot(tb, x) for m, tb, x in zip(ms, tbs, inner)]
            continue
        blocks = [slice(j * b, (j + 1) * b) for j in range(n // b)]
        zero = jnp.zeros((b, n), BF16)
        live = [[j for j in range(n // b) if (j % 2 == 0) == rev] for rev in upper]
        rows = lambda x, js: jnp.concatenate([x[blocks[j], :] for j in js], axis=0)
        inner = [_dot(rows(e, js), tb).astype(BF16) for e, tb, js in zip(es, tbs, live)]
        spread = [jnp.concatenate([x[js.index(j) * b:(js.index(j) + 1) * b, :] if j in js else zero
                                   for j in range(n // b)], axis=0) for x, js in zip(inner, live)]
        upd = [_dot(rows(tb, js), x) for tb, x, js in zip(tbs, spread, live)]
        ms = [jnp.concatenate([m[blocks[j], :] - u[js.index(j) * b:(js.index(j) + 1) * b, :] if j in js
                               else m[blocks[j], :] for j in range(n // b)], axis=0)
              for m, u, js in zip(ms, upd, live)]
    return [m - eye for m in ms]


def _chunk_order(n, reverse):
    return range(n - 1, -1, -1) if reverse else range(n)


def _gdn_chunk_problems(d, q_ref, k_ref, v_ref, p_ref, *, c, with_output):
    reverse = d == 1
    t = k_ref.shape[0]
    sp = p_ref[...]
    gc, tot = _chunk_sums(sp, c, reverse)
    strict, incl = _tri_masks(c, reverse)
    problems = []
    for ci in _chunk_order(t // c, reverse):
        rows = slice(ci * c, (ci + 1) * c)
        gcc = gc[rows, :]
        gct = gcc.T
        totc = tot[rows, :]
        for hh in range(GDN_HEADS):
            cb = COL_BETA + d * GDN_HEADS + hh
            cl = COL_GLOGA + d * GDN_HEADS + hh
            hs = slice(hh * GDN_DK, (hh + 1) * GDN_DK)
            kb = k_ref[rows, hs]
            kf = kb.astype(F32)
            vf = v_ref[rows, hs].astype(F32)
            gcol = gcc[:, cl:cl + 1]
            tcol = totc[:, cl:cl + 1]
            beta = sp[rows, cb:cb + 1]
            e_incl = _decay_matrix(gcol, gct[cl:cl + 1, :], incl)
            if with_output:
                qb = q_ref[rows, hs]
                scores = _dot_nt(jnp.concatenate([kb, qb], axis=0), kb)
                kk = scores[:c]
            else:
                kk = _dot_nt(kb, kb)
            pr = {"d": d, "hh": hh, "rows": rows, "hs": hs,
                  "a": kk * jnp.where(strict, e_incl, 0.0) * beta,
                  "rhs": jnp.concatenate([beta * vf, (beta * jnp.exp(gcol)) * kf], axis=1),
                  "k_w": (kf * jnp.exp(tcol - gcol)).astype(BF16),
                  "c_dec": jnp.exp(tcol[0:1, :])}
            if with_output:
                pr["q_w"] = (qb.astype(F32) * jnp.exp(gcol)).astype(BF16)
                pr["p"] = (scores[c:] * e_incl).astype(BF16)
            problems.append(pr)
    return problems


def _gdn_body(*refs, c, with_output):
    s0_ref = refs[0]
    ins = refs[1:9]
    if with_output:
        outs = refs[9:11]
        st_ref = refs[11]
    else:
        outs = (None, None)
        st_ref = refs[9]

    @pl.when(pl.program_id(1) == 0)
    def _():
        st_ref[...] = s0_ref[...]

    per_dir = [_gdn_chunk_problems(d, *ins[4 * d:4 * d + 4], c=c, with_output=with_output) for d in range(2)]
    n_steps = len(per_dir[0]) // GDN_HEADS
    problems = per_dir[0] + per_dir[1]
    minv = _unit_triangular_inverses_minus_eye([pr["a"] for pr in problems], [pr["d"] == 1 for pr in problems],
                                               _merge_level_masks(c))
    for pr, m in zip(problems, minv):
        x = pr["rhs"] + _dot(m.astype(BF16), pr["rhs"].astype(BF16))
        pr["w_v"] = x[:, :GDN_DV]
        pr["w_k"] = x[:, GDN_DV:].astype(BF16)
    state = {(d, hh): st_ref[d, hh] for d in range(2) for hh in range(GDN_HEADS)}
    for step in range(n_steps):
        now = [pr for prs in per_dir for pr in prs[step * GDN_HEADS:(step + 1) * GDN_HEADS]]
        sbs = [state[pr["d"], pr["hh"]].astype(BF16) for pr in now]
        us = [pr["w_v"] - _dot(pr["w_k"], sb) for pr, sb in zip(now, sbs)]
        ubs = [u.astype(BF16) for u in us]
        if with_output:
            for pr, sb, ub in zip(now, sbs, ubs):
                outs[pr["d"]][pr["rows"], pr["hs"]] = _dot(jnp.concatenate([pr["q_w"], pr["p"]], axis=1),
                                                           jnp.concatenate([sb, ub], axis=0))
        for pr, ub in zip(now, ubs):
            key = (pr["d"], pr["hh"])
            state[key] = pr["c_dec"] * state[key] + _dot_tn(pr["k_w"], ub)
    for (d, hh), s in state.items():
        st_ref[d, hh] = s


def _ret_direction(d, q_ref, k_ref, v_ref, lg_ref, o_ref, st_ref, *, c, with_output):
    reverse = d == 1
    t = q_ref.shape[0]
    r = lax.broadcasted_iota(jnp.int32, (c, c), 0)
    col = lax.broadcasted_iota(jnp.int32, (c, c), 1)
    dist = ((col - r) if reverse else (r - col)).astype(F32)
    incl = dist >= 0.0
    pos = lax.broadcasted_iota(jnp.int32, (c, 1), 0).astype(F32)
    steps = (float(c) - pos) if reverse else (pos + 1.0)
    problems, c_decs, q_dec_of = [], [], []
    for hh in range(RET_HEADS):
        hs = slice(hh * RET_DK, (hh + 1) * RET_DK)
        lg = lg_ref[d, hh]
        v_dec = jnp.exp((float(c) - steps) * lg)
        c_decs.append(jnp.exp(jnp.full((1, 1), float(c), F32) * lg))
        if with_output:
            dmat = jnp.where(incl, jnp.exp(jnp.where(incl, dist * lg, 0.0)), 0.0)
            q_dec_of.append(jnp.exp(steps * lg))
        for ci in _chunk_order(t // c, reverse):
            rows = slice(ci * c, (ci + 1) * c)
            kb = k_ref[rows, hs]
            vb = v_ref[rows, hs]
            pr = {"rows": rows, "hs": hs, "inc": _dot_tn(kb, (vb.astype(F32) * v_dec).astype(BF16))}
            if with_output:
                qb = q_ref[rows, hs]
                pr["qb"] = qb
                pr["intra"] = _dot((_dot_nt(qb, kb) * dmat).astype(BF16), vb)
            problems.append(pr)
    n_steps = t // c
    for hh in range(RET_HEADS):
        s = st_ref[d, hh]
        for pr in problems[hh * n_steps:(hh + 1) * n_steps]:
            if with_output:
                o_ref[pr["rows"], pr["hs"]] = pr["intra"] + _dot(pr["qb"], s.astype(BF16)) * q_dec_of[hh]
            s = c_decs[hh] * s + pr["inc"]
        st_ref[d, hh] = s


def _ret_body(*refs, c, with_output):
    lg_ref, s0_ref = refs[0], refs[1]
    ins = refs[2:8]
    if with_output:
        outs = refs[8:10]
        st_ref = refs[10]
    else:
        outs = (None, None)
        st_ref = refs[8]

    @pl.when(pl.program_id(1) == 0)
    def _():
        st_ref[...] = s0_ref[...]

    for d in range(2):
        q_ref, k_ref, v_ref = ins[3 * d:3 * d + 3]
        _ret_direction(d, q_ref, k_ref, v_ref, lg_ref, outs[d], st_ref, c=c, with_output=with_output)


def _ssd_chunk_problems(d, sel_ref, x_ref, b_ref, c_ref, p_ref, o_ref, *, c, with_output):
    reverse = d == 1
    t = x_ref.shape[0]
    sp = p_ref[...]
    gc, tot = _chunk_sums(sp, c, reverse)
    col = lax.broadcasted_iota(jnp.int32, sp.shape, 1)
    lo = COL_SLOGA + d * SSD_HEADS
    mine = jnp.where(col >= lo, jnp.where(col < lo + SSD_HEADS, 1.0, 0.0), 0.0)
    gcm = gc * mine
    totm = tot * mine
    sel_d = sel_ref[d, 0]
    sel_l = sel_ref[d, 1]
    xd = x_ref[...].astype(F32) * _dot(sp.astype(BF16), sel_d)
    xdb = xd.astype(BF16)
    n_chunks = t // c
    stack = [jnp.exp(totm - gcm).astype(BF16)]
    if with_output:
        stack.append(jnp.exp(gcm).astype(BF16))
    tot_base = len(stack) * t
    for ci in range(n_chunks):
        stack += list(_split3(jnp.exp(totm[ci * c:ci * c + BF16_ROWS, :])))
    expanded = _dot(jnp.concatenate(stack, axis=0), sel_l)
    vw = (xd * expanded[0:t, :]).astype(BF16)
    if with_output:
        e1x = expanded[t:2 * t, :]
        _, incl = _tri_masks(c, reverse)
        lane = lax.broadcasted_iota(jnp.int32, (c, 2 * SSD_HEADDIM), 1)
        first_head = lane < SSD_HEADDIM
    gw = SSD_HG * SSD_HEADDIM
    problems = []
    for ci in _chunk_order(t // c, reverse):
        rows = slice(ci * c, (ci + 1) * c)
        gcc = gc[rows, :]
        gct = gcc.T
        base = tot_base + 3 * BF16_ROWS * ci
        c_dec = (expanded[base:base + 1, :] + expanded[base + BF16_ROWS:base + BF16_ROWS + 1, :]
                 + expanded[base + 2 * BF16_ROWS:base + 2 * BF16_ROWS + 1, :])
        for gg in range(SSD_GROUPS):
            gs = slice(gg * SSD_STATE, (gg + 1) * SSD_STATE)
            gcols = slice(gg * gw, (gg + 1) * gw)
            bb = b_ref[rows, gs]
            pr = {"d": d, "gg": gg, "rows": rows, "gcols": gcols,
                  "inc": _dot_tn(bb, vw[rows, gcols]), "c_dec": c_dec[:, gcols]}
            if with_output:
                cb = c_ref[rows, gs]
                scores = _dot_nt(cb, bb)
                for pair in range(SSD_HG // 2):
                    h0 = gg * SSD_HG + 2 * pair
                    cols = slice(h0 * SSD_HEADDIM, (h0 + 2) * SSD_HEADDIM)
                    ms = []
                    for hd in (h0, h0 + 1):
                        cl = lo + hd
                        dmat = _decay_matrix(gcc[:, cl:cl + 1], gct[cl:cl + 1, :], incl)
                        ms.append((scores * dmat).astype(BF16))
                    xp = xdb[rows, cols]
                    zero = jnp.zeros_like(xp)
                    rhs = jnp.concatenate([jnp.where(first_head, xp, zero), jnp.where(first_head, zero, xp)], axis=0)
                    o_ref[rows, cols] = _dot(jnp.concatenate(ms, axis=1), rhs)
                pr["cb"] = cb
                pr["q_dec"] = e1x[rows, gcols]
            problems.append(pr)
    return problems


def _ssd_body(*refs, c, with_output):
    sel_ref, s0_ref = refs[0], refs[1]
    ins = refs[2:10]
    if with_output:
        outs = refs[10:12]
        st_ref = refs[12]
    else:
        outs = (None, None)
        st_ref = refs[10]

    @pl.when(pl.program_id(1) == 0)
    def _():
        st_ref[...] = s0_ref[...]

    per_dir = [_ssd_chunk_problems(d, sel_ref, *ins[4 * d:4 * d + 4], outs[d], c=c, with_output=with_output)
               for d in range(2)]
    n_steps = len(per_dir[0]) // SSD_GROUPS
    state = {(d, gg): st_ref[d, gg] for d in range(2) for gg in range(SSD_GROUPS)}
    for step in range(n_steps):
        for prs in per_dir:
            for pr in prs[step * SSD_GROUPS:(step + 1) * SSD_GROUPS]:
                key = (pr["d"], pr["gg"])
                if with_output:
                    o_ref = outs[pr["d"]]
                    inter = _dot(pr["cb"], state[key].astype(BF16))
                    o_ref[pr["rows"], pr["gcols"]] = o_ref[pr["rows"], pr["gcols"]] + inter * pr["q_dec"]
                state[key] = pr["c_dec"] * state[key] + pr["inc"]
    for (d, gg), s in state.items():
        st_ref[d, gg] = s


def ssd_head_selectors():
    sel = np.zeros((2, 2, SMALL_W, SSD_DINNER), np.float32)
    heads = np.arange(SSD_DINNER) // SSD_HEADDIM
    for d in range(2):
        sel[d, 0, COL_DELTA + d * SSD_HEADS + heads, np.arange(SSD_DINNER)] = 1.0
        sel[d, 1, COL_SLOGA + d * SSD_HEADS + heads, np.arange(SSD_DINNER)] = 1.0
    return jnp.asarray(sel, BF16)


def _scan_call(body, name, s0, arrays, out_width, with_output, extra_args=(), extra_specs=(), block=SCAN_BLOCK):
    b_, seq, _ = arrays[0].shape
    t = min(block, seq)
    c = min(SCAN_CHUNK, t)
    nt = seq // t
    fwd = lambda w: pl.BlockSpec((None, t, w), lambda b, i: (b, i, 0))
    bwd = lambda w: pl.BlockSpec((None, t, w), lambda b, i: (b, nt - 1 - i, 0))
    st_spec = pl.BlockSpec((None,) + s0.shape[1:], lambda b, i: (b,) + (0,) * (s0.ndim - 1))
    in_specs = list(extra_specs) + [st_spec]
    in_specs += [fwd(a.shape[-1]) for a in arrays] + [bwd(a.shape[-1]) for a in arrays]
    out_specs, out_shape = [], []
    if with_output:
        out_specs += [fwd(out_width), bwd(out_width)]
        out_shape += [jax.ShapeDtypeStruct((b_, seq, out_width), F32)] * 2
    out_specs.append(st_spec)
    out_shape.append(jax.ShapeDtypeStruct(s0.shape, F32))
    res = pl.pallas_call(
        functools.partial(body, c=c, with_output=with_output),
        grid=(b_, nt),
        in_specs=in_specs,
        out_specs=out_specs,
        out_shape=out_shape,
        compiler_params=_cparams(2),
        name=name,
    )(*extra_args, s0, *arrays, *arrays)
    if with_output:
        return res[0], res[1], res[2]
    return None, None, res[0]


def token_mixers(proj, prep, states, lp, with_output):
    gdn_s0, ret_s0, ssd_s0 = states
    a_f, a_b, gdn_s = _scan_call(_gdn_body, "gdn_scan", gdn_s0,
                                 [prep["gq"], prep["gk"], prep["gv"], prep["smallp"]],
                                 GDN_HEADS * GDN_DV, with_output)
    ret_lg = -jnp.exp(lp["ret_decay"].astype(F32))
    sel = ssd_head_selectors()
    b_f, b_b, ret_s = _scan_call(_ret_body, "ret_scan", ret_s0,
                                 [prep["rq"], prep["rk"], proj["r_v"]],
                                 RET_HEADS * RET_DV, with_output, block=RET_SCAN_BLOCK, extra_args=(ret_lg,),
                                 extra_specs=(pl.BlockSpec(memory_space=pltpu.SMEM),))
    c_f, c_b, ssd_s = _scan_call(_ssd_body, "ssd_scan", ssd_s0,
                                 [prep["sx"], prep["sb"], prep["sc"], prep["smallp"]],
                                 SSD_DINNER, with_output, extra_args=(sel,),
                                 extra_specs=(_resident(sel.shape),))
    o = None
    if with_output:
        o = {"a_f": a_f, "a_b": a_b, "b_f": b_f, "b_b": b_b, "c_f": c_f, "c_b": c_b}
    return o, (gdn_s, ret_s, ssd_s)


def zero_states(b_):
    return (jnp.zeros((b_, 2, GDN_HEADS, GDN_DK, GDN_DV), F32),
            jnp.zeros((b_, 2, RET_HEADS, RET_DK, RET_DV), F32),
            jnp.zeros((b_, 2, SSD_GROUPS, SSD_STATE, SSD_HG * SSD_HEADDIM), F32))


def _merge_body(h_ref, mod_ref, oaf_ref, oab_ref, obf_ref, obb_ref, ocf_ref, ocb_ref, sx_ref,
                gz_ref, rg_ref, sz_ref, brg_ref, gng_ref, rng_ref, sd_ref, sng_ref,
                wa_ref, wb_ref, wc_ref, wo_ref, lng_ref, lnb_ref, o_ref, *, alpha):
    d = h_ref.shape[-1]
    oa = oaf_ref[...] + oab_ref[...]
    gz = gz_ref[...].astype(F32)
    ya = []
    for hh in range(GDN_HEADS):
        x = oa[:, hh * GDN_DV:(hh + 1) * GDN_DV]
        ms = jnp.mean(x * x, axis=-1, keepdims=True)
        ya.append(x * lax.rsqrt(ms + RMS_EPS) * gng_ref[...] * gz[:, hh * GDN_DV:(hh + 1) * GDN_DV])
    ya = jnp.concatenate(ya, axis=-1).astype(BF16)
    ob = obf_ref[...] + obb_ref[...]
    rg = rg_ref[...].astype(F32)
    yb = []
    for hh in range(RET_HEADS):
        sl = slice(hh * RET_DV, (hh + 1) * RET_DV)
        x = ob[:, sl]
        mu = jnp.mean(x, axis=-1, keepdims=True)
        xc = x - mu
        var = jnp.mean(xc * xc, axis=-1, keepdims=True)
        yb.append(xc * lax.rsqrt(var + LN_EPS) * rng_ref[:, sl] * rg[:, sl])
    yb = jnp.concatenate(yb, axis=-1).astype(BF16)
    oc = (ocf_ref[...] + ocb_ref[...] + sd_ref[...] * sx_ref[...].astype(F32)) * sz_ref[...].astype(F32)
    gw = SSD_DINNER // SSD_GROUPS
    yc = []
    for gg in range(SSD_GROUPS):
        sl = slice(gg * gw, (gg + 1) * gw)
        x = oc[:, sl]
        ms = jnp.mean(x * x, axis=-1, keepdims=True)
        yc.append(x * lax.rsqrt(ms + RMS_EPS) * sng_ref[:, sl])
    yc = jnp.concatenate(yc, axis=-1).astype(BF16)
    gates = brg_ref[...].astype(F32)
    merged = (gates[:, 0:d] * _dot(ya, wa_ref[...])
              + gates[:, d:2 * d] * _dot(yb, wb_ref[...])
              + gates[:, 2 * d:3 * d] * _dot(yc, wc_ref[...]))
    mix = _dot(merged.astype(BF16), wo_ref[...])
    h = h_ref[...]
    y = alpha * h + mod_ref[MOD_MIX + 2:MOD_MIX + 3, :] * mix
    o_ref[...] = _layer_norm_rows(y, lng_ref[...], lnb_ref[...])


def mixer_merge(h, mod, o, proj, sx, lp, ln_g, ln_b, *, alpha, tm):
    b_, seq, d = h.shape
    tm = min(tm, seq)

    def tok(w):
        return pl.BlockSpec((None, tm, w), lambda b, i: (b, i, 0))

    ssd_d_cols = jnp.repeat(lp["ssd_d"].astype(F32), SSD_HEADDIM).reshape(1, SSD_DINNER)
    args = [h, mod, o["a_f"], o["a_b"], o["b_f"], o["b_b"], o["c_f"], o["c_b"], sx,
            proj["g_z"], proj["r_g"], proj["s_z"], proj["br_gate"],
            lp["gdn_norm_g"].astype(F32).reshape(1, GDN_DV),
            lp["ret_norm_g"].astype(F32).reshape(1, RET_HEADS * RET_DV),
            ssd_d_cols, lp["ssd_norm_g"].astype(F32).reshape(1, SSD_DINNER),
            lp["w_br_a"], lp["w_br_b"], lp["w_br_c"], lp["w_out"], ln_g.reshape(1, d), ln_b.reshape(1, d)]
    in_specs = [tok(d), pl.BlockSpec((None, N_ADA, d), lambda b, i: (b, 0, 0)),
                tok(GDN_HEADS * GDN_DV), tok(GDN_HEADS * GDN_DV),
                tok(RET_HEADS * RET_DV), tok(RET_HEADS * RET_DV),
                tok(SSD_DINNER), tok(SSD_DINNER), tok(SSD_DINNER),
                tok(GDN_HEADS * GDN_DV), tok(RET_HEADS * RET_DV), tok(SSD_DINNER), tok(N_BRANCH * d)]
    lead = (lp["layer"],)
    in_specs += [_resident_slice(a, lead) if a.ndim == 3 else _resident(a.shape) for a in args[13:]]
    return pl.pallas_call(
        functools.partial(_merge_body, alpha=alpha),
        grid=(b_, seq // tm),
        in_specs=in_specs,
        out_specs=tok(d),
        out_shape=jax.ShapeDtypeStruct(h.shape, F32),
        compiler_params=_cparams(2),
        name="mixer_merge",
    )(*args)


def kernel(x, c, ctx, c_ctx, ada_w, ada_b, ln_g, ln_b, ffn_w13, ffn_w2, mix_w_in,
           gdn_conv_w, gdn_a_log, gdn_dt_bias, gdn_norm_g, ret_decay, ret_norm_g,
           ssd_conv_w, ssd_conv_b, ssd_a_log, ssd_dt_bias, ssd_d, ssd_norm_g,
           w_br_a, w_br_b, w_br_c, mix_w_out):
    depth = ada_w.shape[0]
    alpha = float((2 * depth) ** 0.25)
    b_, seq, d = x.shape
    rope2 = rope_tables(seq)
    c_rows = jnp.concatenate([c, c_ctx[None, :]], axis=0)
    h, hc = x, ctx
    w13, w2 = ffn_w13.astype(BF16), ffn_w2.astype(BF16)
    w_in_r = split_w_in(mix_w_in, d)
    w_branch = {'w_br_a': w_br_a.astype(BF16), 'w_br_b': w_br_b.astype(BF16),
                'w_br_c': w_br_c.astype(BF16), 'w_out': mix_w_out.astype(BF16)}
    ada_b3 = ada_b.reshape(depth, 1, -1)
    for i in range(depth):
        last = i == depth - 1
        mod_all = ada_modulation(c_rows, ada_w, ada_b3, i)
        mod = mod_all[:b_].reshape(b_, N_ADA, d)
        mod_c = jnp.broadcast_to(mod_all[b_:].reshape(1, N_ADA, d), (b_, N_ADA, d))
        lp = {'layer': i, 'gdn_conv_w': gdn_conv_w[i], 'gdn_a_log': gdn_a_log[i],
              'gdn_dt_bias': gdn_dt_bias[i], 'gdn_norm_g': gdn_norm_g[i], 'ret_decay': ret_decay[i],
              'ret_norm_g': ret_norm_g[i], 'ssd_conv_w': ssd_conv_w[i], 'ssd_conv_b': ssd_conv_b[i],
              'ssd_a_log': ssd_a_log[i], 'ssd_dt_bias': ssd_dt_bias[i], 'ssd_d': ssd_d[i],
              'ssd_norm_g': ssd_norm_g[i], **w_branch}
        ffn = functools.partial(ffn_sublayer, alpha=alpha, tm=FFN_TILE)
        h = ffn(h, mod, w13, w2, (i, 0), ln_g[i, 0], ln_b[i, 0], mod_base=MOD_FFN1)
        hc = ffn(hc, mod_c, w13, w2, (i, 0), ln_g[i, 0], ln_b[i, 0], mod_base=MOD_FFN1)
        proj_c, prep_c = mixer_in_projection_prep(hc, mod_c, w_in_r, None, lp, tm=INPROJ_TILE)
        o_c, ctx_states = token_mixers(proj_c, prep_c, zero_states(b_), lp, not last)
        proj, prep = mixer_in_projection_prep(h, mod, w_in_r, rope2, lp, tm=INPROJ_TILE)
        o_l, _ = token_mixers(proj, prep, ctx_states, lp, True)
        h = mixer_merge(h, mod, o_l, proj, prep["sx"], lp, ln_g[i, 1], ln_b[i, 1], alpha=alpha, tm=MERGE_TILE)
        h = ffn(h, mod, w13, w2, (i, 1), ln_g[i, 2], ln_b[i, 2], mod_base=MOD_FFN2)
        if not last:
            hc = mixer_merge(hc, mod_c, o_c, proj_c, prep_c["sx"], lp, ln_g[i, 1], ln_b[i, 1],
                             alpha=alpha, tm=MERGE_TILE)
            hc = ffn(hc, mod_c, w13, w2, (i, 1), ln_g[i, 2], ln_b[i, 2], mod_base=MOD_FFN2)
    return h
```

```python
import functools

import numpy as np
import jax
import jax.numpy as jnp
from jax import lax
from jax.experimental import pallas as pl
from jax.experimental.pallas import tpu as pltpu

F32 = jnp.float32
BF16 = jnp.bfloat16

GRID_W = 64
CONV_K = 5
GDN_HEADS = 4
GDN_DK = 128
GDN_DV = 128
RET_HEADS = 4
RET_DK = 128
RET_DV = 128
SSD_HEADS = 16
SSD_HEADDIM = 64
SSD_GROUPS = 2
SSD_STATE = 128
SSD_DINNER = SSD_HEADS * SSD_HEADDIM
SSD_HG = SSD_HEADS // SSD_GROUPS
N_BRANCH = 3
N_ADA = 9
ROPE_BASE = 10000.0
GDN_QKV = GDN_HEADS * (2 * GDN_DK + GDN_DV)
SSD_XBC = SSD_DINNER + 2 * SSD_GROUPS * SSD_STATE
LN_EPS = 1e-5
RMS_EPS = 1e-6
L2_EPS = 1e-6

VMEM_LIMIT_BYTES = 56 * 1024 * 1024
LANE = 128
SMALL_W = LANE
BF16_ROWS = 16
CONV_ROWS = 128
FFN_ROWS = 256
FFN_FF_CHUNK = 1024
FFN_TILE = 1024
INPROJ_TILE = 512
MERGE_TILE = 512
ADA_COLS = 1024
MOD_FFN1, MOD_MIX, MOD_FFN2 = 0, 3, 6
INPROJ_ROWS = 256
SCAN_CHUNK = 128
SCAN_BLOCK = 256
RET_SCAN_BLOCK = 512

COL_BETA = 0
COL_GLOGA = 2 * GDN_HEADS
COL_DELTA = 4 * GDN_HEADS
COL_SLOGA = COL_DELTA + 2 * SSD_HEADS
COL_END = COL_SLOGA + 2 * SSD_HEADS


def _cparams(n_axes):
    return pltpu.CompilerParams(dimension_semantics=("arbitrary",) * n_axes,
                                vmem_limit_bytes=VMEM_LIMIT_BYTES)


def _resident(shape):
    nd = len(shape)
    return pl.BlockSpec(shape, lambda *_: (0,) * nd, pipeline_mode=pl.Buffered(1))


def _resident_slice(arr, lead):
    tail = tuple(arr.shape[len(lead):])
    return pl.BlockSpec((None,) * len(lead) + tail, lambda *_: tuple(lead) + (0,) * len(tail),
                        pipeline_mode=pl.Buffered(1))


def _layer_norm_rows(y, g, b):
    mu = jnp.mean(y, axis=-1, keepdims=True)
    yc = y - mu
    var = jnp.mean(yc * yc, axis=-1, keepdims=True)
    return yc * lax.rsqrt(var + LN_EPS) * g + b


def _silu(x):
    return x * jax.nn.sigmoid(x)


def _softplus(x):
    return jnp.maximum(x, 0.0) + jnp.log1p(jnp.exp(-jnp.abs(x)))


def _dot(a, b):
    return jnp.dot(a, b, preferred_element_type=F32)


def _dot_nt(a, b):
    return lax.dot_general(a, b, (((1,), (1,)), ((), ())), preferred_element_type=F32)


def _dot_tn(a, b):
    return lax.dot_general(a, b, (((0,), (0,)), ((), ())), preferred_element_type=F32)


def _ada_body(c_ref, w_ref, b_ref, o_ref):
    s = _silu(c_ref[...]).astype(BF16)
    o_ref[...] = _dot(s, w_ref[...].astype(BF16)) + b_ref[...]


def ada_modulation(c_rows, w, b, layer):
    r, d = c_rows.shape
    n = w.shape[-1]
    tn = ADA_COLS
    return pl.pallas_call(
        _ada_body,
        grid=(n // tn,),
        in_specs=[pl.BlockSpec((r, d), lambda j: (0, 0)),
                  pl.BlockSpec((None, d, tn), lambda j: (layer, 0, j)),
                  pl.BlockSpec((None, 1, tn), lambda j: (layer, 0, j))],
        out_specs=pl.BlockSpec((r, tn), lambda j: (0, j)),
        out_shape=jax.ShapeDtypeStruct((r, n), F32),
        compiler_params=_cparams(1),
        name="ada_modulation",
    )(c_rows, w, b)


def _ffn_body(h_ref, mod_ref, w13_ref, w2_ref, lng_ref, lnb_ref, o_ref, *, alpha, ff, chunks, mod_base):
    shift = mod_ref[mod_base:mod_base + 1, :]
    scale = mod_ref[mod_base + 1:mod_base + 2, :]
    gate = mod_ref[mod_base + 2:mod_base + 3, :]
    tm = h_ref.shape[0]
    rb = min(tm, FFN_ROWS)
    for r0 in range(0, tm, rb):
        h = h_ref[r0:r0 + rb, :]
        u = (h * (1.0 + scale) + shift).astype(BF16)
        acc = None
        for c0, c1 in chunks:
            a = _dot(u, w13_ref[:, c0:c1])
            b = _dot(u, w13_ref[:, ff + c0:ff + c1])
            g = (_silu(a) * b).astype(BF16)
            p = _dot(g, w2_ref[c0:c1, :])
            acc = p if acc is None else acc + p
        y = alpha * h + (0.5 * gate) * acc
        o_ref[r0:r0 + rb, :] = _layer_norm_rows(y, lng_ref[...], lnb_ref[...])


def _ff_chunks(ff, width=FFN_FF_CHUNK):
    return tuple((c0, min(c0 + width, ff)) for c0 in range(0, ff, width))


def ffn_sublayer(h, mod, w13, w2, which, ln_g, ln_b, *, alpha, mod_base, tm):
    b_, seq, d = h.shape
    ff = w2.shape[-2]
    tm = min(tm, seq)
    body = functools.partial(_ffn_body, alpha=alpha, ff=ff, chunks=_ff_chunks(ff), mod_base=mod_base)
    return pl.pallas_call(
        body,
        grid=(b_, seq // tm),
        in_specs=[pl.BlockSpec((None, tm, d), lambda b, i: (b, i, 0)),
                  pl.BlockSpec((None, N_ADA, d), lambda b, i: (b, 0, 0)),
                  _resident_slice(w13, which), _resident_slice(w2, which),
                  _resident((1, d)), _resident((1, d))],
        out_specs=pl.BlockSpec((None, tm, d), lambda b, i: (b, i, 0)),
        out_shape=jax.ShapeDtypeStruct(h.shape, F32),
        compiler_params=_cparams(2),
        name="ffn_sublayer",
    )(h, mod, w13, w2, ln_g.reshape(1, d), ln_b.reshape(1, d))


W_IN_PIECES = ((("g_qkv", GDN_QKV), ("g_z", GDN_HEADS * GDN_DV)),
               (("r_q", RET_HEADS * RET_DK), ("r_k", RET_HEADS * RET_DK), ("r_v", RET_HEADS * RET_DV),
                ("r_g", RET_HEADS * RET_DV), ("s_z", SSD_DINNER), ("s_xbc", SSD_XBC)),
               (("br_gate", None),),
               (("small", SMALL_W),))


def split_w_in(w_in, d):
    sizes = (GDN_QKV, GDN_HEADS * GDN_DV, 2 * GDN_HEADS, 2 * GDN_HEADS,
             RET_HEADS * RET_DK, RET_HEADS * RET_DK, RET_HEADS * RET_DV, RET_HEADS * RET_DV,
             SSD_DINNER, SSD_XBC, 2 * SSD_HEADS, N_BRANCH * d)
    offs = np.concatenate([[0], np.cumsum(sizes)])
    w_in = w_in.astype(BF16)
    n_small = 4 * GDN_HEADS + 2 * SSD_HEADS
    pad = jnp.zeros(w_in.shape[:-1] + (SMALL_W - n_small,), w_in.dtype)
    small = jnp.concatenate([w_in[..., offs[2]:offs[4]], w_in[..., offs[10]:offs[11]], pad], axis=-1)
    return (w_in[..., offs[0]:offs[2]], w_in[..., offs[4]:offs[10]], w_in[..., offs[11]:offs[12]], small)


F32_ROWS = 8
def _w_in_offsets(d):
    offs = {}
    for piece, groups in enumerate(W_IN_PIECES):
        c0 = 0
        for name, width in groups:
            width = N_BRANCH * d if width is None else width
            offs[name] = (piece, c0, c0 + width)
            c0 += width
    return offs


def _conv_silu_rows(raw, tm, w_ref, b_ref, first, last):
    h8 = F32_ROWS
    xx = jnp.concatenate([jnp.where(first, 0.0, raw[tm:tm + h8, :]), raw[0:tm, :],
                          jnp.where(last, 0.0, raw[tm + h8:tm + 2 * h8, :])], axis=0).astype(BF16)
    rb = min(tm, CONV_ROWS)
    win = rb + 2 * h8
    r = lax.broadcasted_iota(jnp.int32, (rb, CONV_K * win), 0)
    col = lax.broadcasted_iota(jnp.int32, (rb, CONV_K * win), 1)
    tap = jnp.zeros_like(col)
    for k in range(1, CONV_K):
        tap = tap + jnp.where(col >= k * win, 1, 0)
    shifts = jnp.where(col == r + tap * (win + 1) + (h8 - CONV_K // 2), 1.0, 0.0).astype(BF16)
    wb = [w_ref[k:k + 1, :].astype(BF16) for k in range(CONV_K)]
    out = []
    for blk in range(tm // rb):
        window = xx[blk * rb:blk * rb + win, :]
        out.append(_dot(shifts, jnp.concatenate([window * wk for wk in wb], axis=0)))
    acc = out[0] if len(out) == 1 else jnp.concatenate(out, axis=0)
    if b_ref is not None:
        acc = acc + b_ref[...]
    return _silu(acc)


def _inproj_prep_body(*refs, use_rope, offs):
    hp_ref, h_ref, hn_ref, mod_ref = refs[:4]
    w_refs = refs[4:4 + len(W_IN_PIECES)]
    k = 4 + len(W_IN_PIECES)
    if use_rope:
        cos_ref, sin_ref = refs[k:k + 2]
        k += 2
    gw_ref, sw_ref, sb_ref, bias_ref, alog_ref = refs[k:k + 5]
    (gq_o, gk_o, gv_o, gz_o, rq_o, rk_o, rv_o, rg_o, sz_o, sx_o, sb_o, sc_o, brg_o, sm_o) = refs[k + 5:]
    i = pl.program_id(1)
    tile_first = i == 0
    tile_last = i == pl.num_programs(1) - 1
    tile = h_ref.shape[0]
    tm = min(tile, INPROJ_ROWS)
    h8 = F32_ROWS
    shift = mod_ref[MOD_MIX:MOD_MIX + 1, :]
    scale = mod_ref[MOD_MIX + 1:MOD_MIX + 2, :]

    def proj(lhs, name):
        piece, c0, c1 = offs[name]
        return _dot(lhs, w_refs[piece][:, c0:c1])

    for r0 in range(0, tile, tm):
        rows = slice(r0, r0 + tm)
        before = hp_ref[...] if r0 == 0 else h_ref[r0 - h8:r0, :]
        after = hn_ref[...] if r0 + tm == tile else h_ref[r0 + tm:r0 + tm + h8, :]
        first = tile_first if r0 == 0 else False
        last = tile_last if r0 + tm == tile else False
        u_ext = (jnp.concatenate([h_ref[rows, :], before, after], axis=0) * (1.0 + scale) + shift).astype(BF16)
        u = u_ext[0:tm, :]
        raw_g = proj(u_ext, "g_qkv")
        raw_s = proj(u_ext, "s_xbc")

        qkv = _conv_silu_rows(raw_g, tm, gw_ref, None, first, last)
        rq = proj(u, "r_q")
        rk = proj(u, "r_k") * (RET_DK ** -0.5)
        nq = GDN_HEADS * GDN_DK
        for hh in range(GDN_HEADS):
            sl = slice(hh * GDN_DK, (hh + 1) * GDN_DK)
            q = qkv[:, sl]
            kk = qkv[:, nq + hh * GDN_DK: nq + (hh + 1) * GDN_DK]
            q = q * lax.rsqrt(jnp.sum(q * q, axis=-1, keepdims=True) + L2_EPS) * (GDN_DK ** -0.5)
            kk = kk * lax.rsqrt(jnp.sum(kk * kk, axis=-1, keepdims=True) + L2_EPS)
            gq_o[rows, sl] = q.astype(BF16)
            gk_o[rows, sl] = kk.astype(BF16)
        gv_o[rows, :] = qkv[:, 2 * nq:].astype(BF16)

        xbc = _conv_silu_rows(raw_s, tm, sw_ref, sb_ref, first, last)
        sm = proj(u, "small")
        rv_o[rows, :] = proj(u, "r_v").astype(BF16)
        gz_o[rows, :] = _silu(proj(u, "g_z")).astype(BF16)
        sx_o[rows, :] = xbc[:, :SSD_DINNER].astype(BF16)
        sb_o[rows, :] = xbc[:, SSD_DINNER:SSD_DINNER + SSD_GROUPS * SSD_STATE].astype(BF16)
        sc_o[rows, :] = xbc[:, SSD_DINNER + SSD_GROUPS * SSD_STATE:].astype(BF16)

        rg_o[rows, :] = _silu(proj(u, "r_g")).astype(BF16)
        sz_o[rows, :] = _silu(proj(u, "s_z")).astype(BF16)
        if use_rope:
            cos2 = cos_ref[rows, :]
            sin2 = sin_ref[rows, :]
            for hh in range(RET_HEADS):
                sl = slice(hh * RET_DK, (hh + 1) * RET_DK)
                for src, dst in ((rq, rq_o), (rk, rk_o)):
                    xh = src[:, sl]
                    dst[rows, sl] = (xh * cos2 + pltpu.roll(xh, RET_DK // 2, axis=1) * sin2).astype(BF16)
        else:
            rq_o[rows, :] = rq.astype(BF16)
            rk_o[rows, :] = rk.astype(BF16)

        col = lax.broadcasted_iota(jnp.int32, sm.shape, 1)
        shifted = pltpu.roll(sm, COL_SLOGA - COL_DELTA, axis=1)
        xin = jnp.where(col < COL_SLOGA, sm, shifted)
        sp = _softplus(xin + bias_ref[...])
        neg_a = -jnp.exp(alog_ref[...])
        sm_o[rows, :] = jnp.where(col < COL_GLOGA, jax.nn.sigmoid(sm),
                                  jnp.where(col < COL_DELTA, neg_a * sp,
                                            jnp.where(col < COL_SLOGA, sp,
                                                      jnp.where(col < COL_END, sp * neg_a, 0.0))))
        brg_o[rows, :] = jax.nn.sigmoid(proj(u, "br_gate")).astype(BF16)


def mixer_in_projection_prep(h, mod, w_in_r, rope2, lp, *, tm):
    b_, seq, d = h.shape
    tm = min(tm, seq)
    nh = tm // F32_ROWS
    n_halo = seq // F32_ROWS
    use_rope = rope2 is not None
    offs = _w_in_offsets(d)

    def tok(w):
        return pl.BlockSpec((None, tm, w), lambda b, i: (b, i, 0))

    zeros = lambda n: jnp.zeros((n,), F32)
    bias_row = jnp.concatenate([zeros(COL_GLOGA), lp["gdn_dt_bias"].astype(F32).reshape(-1),
                                lp["ssd_dt_bias"].astype(F32).reshape(-1),
                                lp["ssd_dt_bias"].astype(F32).reshape(-1),
                                zeros(SMALL_W - COL_END)]).reshape(1, SMALL_W)
    alog_row = jnp.concatenate([zeros(COL_GLOGA), lp["gdn_a_log"].astype(F32).reshape(-1),
                                zeros(COL_SLOGA - COL_DELTA), lp["ssd_a_log"].astype(F32).reshape(-1),
                                zeros(SMALL_W - COL_END)]).reshape(1, SMALL_W)
    args = [h, h, h, mod, *w_in_r]
    in_specs = [pl.BlockSpec((None, F32_ROWS, d), lambda b, i: (b, jnp.maximum(i * nh - 1, 0), 0)),
                tok(d),
                pl.BlockSpec((None, F32_ROWS, d), lambda b, i: (b, jnp.minimum((i + 1) * nh, n_halo - 1), 0)),
                pl.BlockSpec((None, N_ADA, d), lambda b, i: (b, 0, 0)),
                *[_resident_slice(w, (lp["layer"],)) for w in w_in_r]]
    if use_rope:
        args += [rope2[0], rope2[1]]
        in_specs += [pl.BlockSpec((tm, RET_DK), lambda b, i: (i, 0))] * 2
    consts = [lp["gdn_conv_w"].astype(F32), lp["ssd_conv_w"].astype(F32),
              lp["ssd_conv_b"].astype(F32).reshape(1, SSD_XBC), bias_row, alog_row]
    args += consts
    in_specs += [_resident(a.shape) for a in consts]
    names = ("gq", "gk", "gv", "g_z", "rq", "rk", "r_v", "r_g", "s_z", "sx", "sb", "sc", "br_gate", "smallp")
    hw = GDN_HEADS * GDN_DK
    widths = (hw, hw, GDN_HEADS * GDN_DV, GDN_HEADS * GDN_DV, RET_HEADS * RET_DK, RET_HEADS * RET_DK,
              RET_HEADS * RET_DV, RET_HEADS * RET_DV, SSD_DINNER, SSD_DINNER, SSD_GROUPS * SSD_STATE,
              SSD_GROUPS * SSD_STATE, N_BRANCH * d, SMALL_W)
    dtypes = (BF16,) * 13 + (F32,)
    outs = pl.pallas_call(
        functools.partial(_inproj_prep_body, use_rope=use_rope, offs=offs),
        grid=(b_, seq // tm),
        in_specs=in_specs,
        out_specs=[tok(w) for w in widths],
        out_shape=[jax.ShapeDtypeStruct((b_, seq, w), t) for w, t in zip(widths, dtypes)],
        compiler_params=_cparams(2),
        name="mixer_in_projection_prep",
    )(*args)
    res = dict(zip(names, outs))
    gates = {n: res[n] for n in ("g_z", "r_v", "r_g", "s_z", "br_gate")}
    prep = {n: res[n] for n in ("gq", "gk", "gv", "rq", "rk", "sx", "sb", "sc", "smallp")}
    return gates, prep


def rope_tables(seq_len):
    rows = seq_len // GRID_W
    row_id = jnp.repeat(jnp.arange(rows, dtype=F32), GRID_W)
    col_id = jnp.tile(jnp.arange(GRID_W, dtype=F32), rows)
    n_freq = RET_DK // 4
    inv_freq = ROPE_BASE ** (-jnp.arange(n_freq, dtype=F32) / n_freq)
    ang = jnp.concatenate([row_id[:, None] * inv_freq, col_id[:, None] * inv_freq], axis=-1)
    cos, sin = jnp.cos(ang), jnp.sin(ang)
    return jnp.concatenate([cos, cos], axis=-1), jnp.concatenate([-sin, sin], axis=-1)


def _tri_masks(c, reverse):
    r = lax.broadcasted_iota(jnp.int32, (c, c), 0)
    col = lax.broadcasted_iota(jnp.int32, (c, c), 1)
    if reverse:
        return col > r, col >= r
    return col < r, col <= r


def _split3(x):
    hi = x.astype(BF16)
    r1 = x - hi.astype(F32)
    mid = r1.astype(BF16)
    lo = (r1 - mid.astype(F32)).astype(BF16)
    return hi, mid, lo


def _chunk_sums(x, c, reverse):
    t, w = x.shape
    r = lax.broadcasted_iota(jnp.int32, (c, c), 0)
    col = lax.broadcasted_iota(jnp.int32, (c, c), 1)
    m_cum = jnp.where((col >= r) if reverse else (col <= r), 1.0, 0.0).astype(BF16)
    cums, tots = [], []
    for ci in range(t // c):
        three = _dot(m_cum, jnp.concatenate(_split3(x[ci * c:(ci + 1) * c, :]), axis=1))
        cum = three[:, 0:w] + three[:, w:2 * w] + three[:, 2 * w:3 * w]
        end = 0 if reverse else c - 1
        cums.append(cum)
        tots.append(jnp.broadcast_to(cum[end:end + 1, :], (c, w)))
    if len(cums) == 1:
        return cums[0], tots[0]
    return jnp.concatenate(cums, axis=0), jnp.concatenate(tots, axis=0)


def _decay_matrix(gcol, grow, incl):
    return jnp.where(incl, jnp.exp(jnp.where(incl, gcol - grow, 0.0)), 0.0)


def _merge_level_masks(c):
    r = lax.broadcasted_iota(jnp.int32, (c, c), 0)
    col = lax.broadcasted_iota(jnp.int32, (c, c), 1)
    x = lax.bitwise_xor(r, col)
    return [lax.shift_right_logical(x, k) == 1 for k in range(int(np.log2(c)))]


def _unit_triangular_inverses_minus_eye(a_mats, upper, levels):
    shape = a_mats[0].shape
    eye = jnp.where(lax.broadcasted_iota(jnp.int32, shape, 0) == lax.broadcasted_iota(jnp.int32, shape, 1),
                    1.0, 0.0)
    n = shape[0]
    ms = [eye - jnp.where(levels[0], a, 0.0) for a in a_mats]
    for k, lvl in enumerate(levels[1:], start=1):
        b = 1 << k
        tbs = [m.astype(BF16) for m in ms]
        es = [jnp.where(lvl, a, 0.0).astype(BF16) for a in a_mats]
        if b % BF16_ROWS:
            inner = [_dot(e, tb).astype(BF16) for e, tb in zip(es, tbs)]
            ms = [m - _dot(tb, x) for m, tb, x in zip(ms, tbs, inner)]
            continue
        blocks = [slice(j * b, (j + 1) * b) for j in range(n // b)]
        zero = jnp.zeros((b, n), BF16)
        live = [[j for j in range(n // b) if (j % 2 == 0) == rev] for rev in upper]
        rows = lambda x, js: jnp.concatenate([x[blocks[j], :] for j in js], axis=0)
        inner = [_dot(rows(e, js), tb).astype(BF16) for e, tb, js in zip(es, tbs, live)]
        spread = [jnp.concatenate([x[js.index(j) * b:(js.index(j) + 1) * b, :] if j in js else zero
                                   for j in range(n // b)], axis=0) for x, js in zip(inner, live)]
        upd = [_dot(rows(tb, js), x) for tb, x, js in zip(tbs, spread, live)]
        ms = [jnp.concatenate([m[blocks[j], :] - u[js.index(j) * b:(js.index(j) + 1) * b, :] if j in js
                               else m[blocks[j], :] for j in range(n // b)], axis=0)
              for m, u, js in zip(ms, upd, live)]
    return [m - eye for m in ms]


def _chunk_order(n, reverse):
    return range(n - 1, -1, -1) if reverse else range(n)


def _gdn_chunk_problems(d, q_ref, k_ref, v_ref, p_ref, *, c, with_output):
    reverse = d == 1
    t = k_ref.shape[0]
    sp = p_ref[...]
    gc, tot = _chunk_sums(sp, c, reverse)
    strict, incl = _tri_masks(c, reverse)
    problems = []
    for ci in _chunk_order(t // c, reverse):
        rows = slice(ci * c, (ci + 1) * c)
        gcc = gc[rows, :]
        gct = gcc.T
        totc = tot[rows, :]
        for hh in range(GDN_HEADS):
            cb = COL_BETA + d * GDN_HEADS + hh
            cl = COL_GLOGA + d * GDN_HEADS + hh
            hs = slice(hh * GDN_DK, (hh + 1) * GDN_DK)
            kb = k_ref[rows, hs]
            kf = kb.astype(F32)
            vf = v_ref[rows, hs].astype(F32)
            gcol = gcc[:, cl:cl + 1]
            tcol = totc[:, cl:cl + 1]
            beta = sp[rows, cb:cb + 1]
            e_incl = _decay_matrix(gcol, gct[cl:cl + 1, :], incl)
            if with_output:
                qb = q_ref[rows, hs]
                scores = _dot_nt(jnp.concatenate([kb, qb], axis=0), kb)
                kk = scores[:c]
            else:
                kk = _dot_nt(kb, kb)
            pr = {"d": d, "hh": hh, "rows": rows, "hs": hs,
                  "a": kk * jnp.where(strict, e_incl, 0.0) * beta,
                  "rhs": jnp.concatenate([beta * vf, (beta * jnp.exp(gcol)) * kf], axis=1),
                  "k_w": (kf * jnp.exp(tcol - gcol)).astype(BF16),
                  "c_dec": jnp.exp(tcol[0:1, :])}
            if with_output:
                pr["q_w"] = (qb.astype(F32) * jnp.exp(gcol)).astype(BF16)
                pr["p"] = (scores[c:] * e_incl).astype(BF16)
            problems.append(pr)
    return problems


def _gdn_body(*refs, c, with_output):
    s0_ref = refs[0]
    ins = refs[1:9]
    if with_output:
        outs = refs[9:11]
        st_ref = refs[11]
    else:
        outs = (None, None)
        st_ref = refs[9]

    @pl.when(pl.program_id(1) == 0)
    def _():
        st_ref[...] = s0_ref[...]

    per_dir = [_gdn_chunk_problems(d, *ins[4 * d:4 * d + 4], c=c, with_output=with_output) for d in range(2)]
    n_steps = len(per_dir[0]) // GDN_HEADS
    problems = per_dir[0] + per_dir[1]
    minv = _unit_triangular_inverses_minus_eye([pr["a"] for pr in problems], [pr["d"] == 1 for pr in problems],
                                               _merge_level_masks(c))
    for pr, m in zip(problems, minv):
        x = pr["rhs"] + _dot(m.astype(BF16), pr["rhs"].astype(BF16))
        pr["w_v"] = x[:, :GDN_DV]
        pr["w_k"] = x[:, GDN_DV:].astype(BF16)
    state = {(d, hh): st_ref[d, hh] for d in range(2) for hh in range(GDN_HEADS)}
    for step in range(n_steps):
        now = [pr for prs in per_dir for pr in prs[step * GDN_HEADS:(step + 1) * GDN_HEADS]]
        sbs = [state[pr["d"], pr["hh"]].astype(BF16) for pr in now]
        us = [pr["w_v"] - _dot(pr["w_k"], sb) for pr, sb in zip(now, sbs)]
        ubs = [u.astype(BF16) for u in us]
        if with_output:
            for pr, sb, ub in zip(now, sbs, ubs):
                outs[pr["d"]][pr["rows"], pr["hs"]] = _dot(jnp.concatenate([pr["q_w"], pr["p"]], axis=1),
                                                           jnp.concatenate([sb, ub], axis=0))
        for pr, ub in zip(now, ubs):
            key = (pr["d"], pr["hh"])
            state[key] = pr["c_dec"] * state[key] + _dot_tn(pr["k_w"], ub)
    for (d, hh), s in state.items():
        st_ref[d, hh] = s


def _ret_direction(d, q_ref, k_ref, v_ref, lg_ref, o_ref, st_ref, *, c, with_output):
    reverse = d == 1
    t = q_ref.shape[0]
    r = lax.broadcasted_iota(jnp.int32, (c, c), 0)
    col = lax.broadcasted_iota(jnp.int32, (c, c), 1)
    dist = ((col - r) if reverse else (r - col)).astype(F32)
    incl = dist >= 0.0
    pos = lax.broadcasted_iota(jnp.int32, (c, 1), 0).astype(F32)
    steps = (float(c) - pos) if reverse else (pos + 1.0)
    problems, c_decs, q_dec_of = [], [], []
    for hh in range(RET_HEADS):
        hs = slice(hh * RET_DK, (hh + 1) * RET_DK)
        lg = lg_ref[d, hh]
        v_dec = jnp.exp((float(c) - steps) * lg)
        c_decs.append(jnp.exp(jnp.full((1, 1), float(c), F32) * lg))
        if with_output:
            dmat = jnp.where(incl, jnp.exp(jnp.where(incl, dist * lg, 0.0)), 0.0)
            q_dec_of.append(jnp.exp(steps * lg))
        for ci in _chunk_order(t // c, reverse):
            rows = slice(ci * c, (ci + 1) * c)
            kb = k_ref[rows, hs]
            vb = v_ref[rows, hs]
            pr = {"rows": rows, "hs": hs, "inc": _dot_tn(kb, (vb.astype(F32) * v_dec).astype(BF16))}
            if with_output:
                qb = q_ref[rows, hs]
                pr["qb"] = qb
                pr["intra"] = _dot((_dot_nt(qb, kb) * dmat).astype(BF16), vb)
            problems.append(pr)
    n_steps = t // c
    for hh in range(RET_HEADS):
        s = st_ref[d, hh]
        for pr in problems[hh * n_steps:(hh + 1) * n_steps]:
            if with_output:
                o_ref[pr["rows"], pr["hs"]] = pr["intra"] + _dot(pr["qb"], s.astype(BF16)) * q_dec_of[hh]
            s = c_decs[hh] * s + pr["inc"]
        st_ref[d, hh] = s


def _ret_body(*refs, c, with_output):
    lg_ref, s0_ref = refs[0], refs[1]
    ins = refs[2:8]
    if with_output:
        outs = refs[8:10]
        st_ref = refs[10]
    else:
        outs = (None, None)
        st_ref = refs[8]

    @pl.when(pl.program_id(1) == 0)
    def _():
        st_ref[...] = s0_ref[...]

    for d in range(2):
        q_ref, k_ref, v_ref = ins[3 * d:3 * d + 3]
        _ret_direction(d, q_ref, k_ref, v_ref, lg_ref, outs[d], st_ref, c=c, with_output=with_output)


def _ssd_chunk_problems(d, sel_ref, x_ref, b_ref, c_ref, p_ref, o_ref, *, c, with_output):
    reverse = d == 1
    t = x_ref.shape[0]
    sp = p_ref[...]
    gc, tot = _chunk_sums(sp, c, reverse)
    col = lax.broadcasted_iota(jnp.int32, sp.shape, 1)
    lo = COL_SLOGA + d * SSD_HEADS
    mine = jnp.where(col >= lo, jnp.where(col < lo + SSD_HEADS, 1.0, 0.0), 0.0)
    gcm = gc * mine
    totm = tot * mine
    sel_d = sel_ref[d, 0]
    sel_l = sel_ref[d, 1]
    xd = x_ref[...].astype(F32) * _dot(sp.astype(BF16), sel_d)
    xdb = xd.astype(BF16)
    n_chunks = t // c
    stack = [jnp.exp(totm - gcm).astype(BF16)]
    if with_output:
        stack.append(jnp.exp(gcm).astype(BF16))
    tot_base = len(stack) * t
    for ci in range(n_chunks):
        stack += list(_split3(jnp.exp(totm[ci * c:ci * c + BF16_ROWS, :])))
    expanded = _dot(jnp.concatenate(stack, axis=0), sel_l)
    vw = (xd * expanded[0:t, :]).astype(BF16)
    if with_output:
        e1x = expanded[t:2 * t, :]
        _, incl = _tri_masks(c, reverse)
        lane = lax.broadcasted_iota(jnp.int32, (c, 2 * SSD_HEADDIM), 1)
        first_head = lane < SSD_HEADDIM
    gw = SSD_HG * SSD_HEADDIM
    problems = []
    for ci in _chunk_order(t // c, reverse):
        rows = slice(ci * c, (ci + 1) * c)
        gcc = gc[rows, :]
        gct = gcc.T
        base = tot_base + 3 * BF16_ROWS * ci
        c_dec = (expanded[base:base + 1, :] + expanded[base + BF16_ROWS:base + BF16_ROWS + 1, :]
                 + expanded[base + 2 * BF16_ROWS:base + 2 * BF16_ROWS + 1, :])
        for gg in range(SSD_GROUPS):
            gs = slice(gg * SSD_STATE, (gg + 1) * SSD_STATE)
            gcols = slice(gg * gw, (gg + 1) * gw)
            bb = b_ref[rows, gs]
            pr = {"d": d, "gg": gg, "rows": rows, "gcols": gcols,
                  "inc": _dot_tn(bb, vw[rows, gcols]), "c_dec": c_dec[:, gcols]}
            if with_output:
                cb = c_ref[rows, gs]
                scores = _dot_nt(cb, bb)
                for pair in range(SSD_HG // 2):
                    h0 = gg * SSD_HG + 2 * pair
                    cols = slice(h0 * SSD_HEADDIM, (h0 + 2) * SSD_HEADDIM)
                    ms = []
                    for hd in (h0, h0 + 1):
                        cl = lo + hd
                        dmat = _decay_matrix(gcc[:, cl:cl + 1], gct[cl:cl + 1, :], incl)
                        ms.append((scores * dmat).astype(BF16))
                    xp = xdb[rows, cols]
                    zero = jnp.zeros_like(xp)
                    rhs = jnp.concatenate([jnp.where(first_head, xp, zero), jnp.where(first_head, zero, xp)], axis=0)
                    o_ref[rows, cols] = _dot(jnp.concatenate(ms, axis=1), rhs)
                pr["cb"] = cb
                pr["q_dec"] = e1x[rows, gcols]
            problems.append(pr)
    return problems


def _ssd_body(*refs, c, with_output):
    sel_ref, s0_ref = refs[0], refs[1]
    ins = refs[2:10]
    if with_output:
        outs = refs[10:12]
        st_ref = refs[12]
    else:
        outs = (None, None)
        st_ref = refs[10]

    @pl.when(pl.program_id(1) == 0)
    def _():
        st_ref[...] = s0_ref[...]

    per_dir = [_ssd_chunk_problems(d, sel_ref, *ins[4 * d:4 * d + 4], outs[d], c=c, with_output=with_output)
               for d in range(2)]
    n_steps = len(per_dir[0]) // SSD_GROUPS
    state = {(d, gg): st_ref[d, gg] for d in range(2) for gg in range(SSD_GROUPS)}
    for step in range(n_steps):
        for prs in per_dir:
            for pr in prs[step * SSD_GROUPS:(step + 1) * SSD_GROUPS]:
                key = (pr["d"], pr["gg"])
                if with_output:
                    o_ref = outs[pr["d"]]
                    inter = _dot(pr["cb"], state[key].astype(BF16))
                    o_ref[pr["rows"], pr["gcols"]] = o_ref[pr["rows"], pr["gcols"]] + inter * pr["q_dec"]
                state[key] = pr["c_dec"] * state[key] + pr["inc"]
    for (d, gg), s in state.items():
        st_ref[d, gg] = s


def ssd_head_selectors():
    sel = np.zeros((2, 2, SMALL_W, SSD_DINNER), np.float32)
    heads = np.arange(SSD_DINNER) // SSD_HEADDIM
    for d in range(2):
        sel[d, 0, COL_DELTA + d * SSD_HEADS + heads, np.arange(SSD_DINNER)] = 1.0
        sel[d, 1, COL_SLOGA + d * SSD_HEADS + heads, np.arange(SSD_DINNER)] = 1.0
    return jnp.asarray(sel, BF16)


def _scan_call(body, name, s0, arrays, out_width, with_output, extra_args=(), extra_specs=(), block=SCAN_BLOCK):
    b_, seq, _ = arrays[0].shape
    t = min(block, seq)
    c = min(SCAN_CHUNK, t)
    nt = seq // t
    fwd = lambda w: pl.BlockSpec((None, t, w), lambda b, i: (b, i, 0))
    bwd = lambda w: pl.BlockSpec((None, t, w), lambda b, i: (b, nt - 1 - i, 0))
    st_spec = pl.BlockSpec((None,) + s0.shape[1:], lambda b, i: (b,) + (0,) * (s0.ndim - 1))
    in_specs = list(extra_specs) + [st_spec]
    in_specs += [fwd(a.shape[-1]) for a in arrays] + [bwd(a.shape[-1]) for a in arrays]
    out_specs, out_shape = [], []
    if with_output:
        out_specs += [fwd(out_width), bwd(out_width)]
        out_shape += [jax.ShapeDtypeStruct((b_, seq, out_width), F32)] * 2
    out_specs.append(st_spec)
    out_shape.append(jax.ShapeDtypeStruct(s0.shape, F32))
    res = pl.pallas_call(
        functools.partial(body, c=c, with_output=with_output),
        grid=(b_, nt),
        in_specs=in_specs,
        out_specs=out_specs,
        out_shape=out_shape,
        compiler_params=_cparams(2),
        name=name,
    )(*extra_args, s0, *arrays, *arrays)
    if with_output:
        return res[0], res[1], res[2]
    return None, None, res[0]


def token_mixers(proj, prep, states, lp, with_output):
    gdn_s0, ret_s0, ssd_s0 = states
    a_f, a_b, gdn_s = _scan_call(_gdn_body, "gdn_scan", gdn_s0,
                                 [prep["gq"], prep["gk"], prep["gv"], prep["smallp"]],
                                 GDN_HEADS * GDN_DV, with_output)
    ret_lg = -jnp.exp(lp["ret_decay"].astype(F32))
    sel = ssd_head_selectors()
    b_f, b_b, ret_s = _scan_call(_ret_body, "ret_scan", ret_s0,
                                 [prep["rq"], prep["rk"], proj["r_v"]],
                                 RET_HEADS * RET_DV, with_output, block=RET_SCAN_BLOCK, extra_args=(ret_lg,),
                                 extra_specs=(pl.BlockSpec(memory_space=pltpu.SMEM),))
    c_f, c_b, ssd_s = _scan_call(_ssd_body, "ssd_scan", ssd_s0,
                                 [prep["sx"], prep["sb"], prep["sc"], prep["smallp"]],
                                 SSD_DINNER, with_output, extra_args=(sel,),
                                 extra_specs=(_resident(sel.shape),))
    o = None
    if with_output:
        o = {"a_f": a_f, "a_b": a_b, "b_f": b_f, "b_b": b_b, "c_f": c_f, "c_b": c_b}
    return o, (gdn_s, ret_s, ssd_s)


def zero_states(b_):
    return (jnp.zeros((b_, 2, GDN_HEADS, GDN_DK, GDN_DV), F32),
            jnp.zeros((b_, 2, RET_HEADS, RET_DK, RET_DV), F32),
            jnp.zeros((b_, 2, SSD_GROUPS, SSD_STATE, SSD_HG * SSD_HEADDIM), F32))


def _merge_body(h_ref, mod_ref, oaf_ref, oab_ref, obf_ref, obb_ref, ocf_ref, ocb_ref, sx_ref,
                gz_ref, rg_ref, sz_ref, brg_ref, gng_ref, rng_ref, sd_ref, sng_ref,
                wa_ref, wb_ref, wc_ref, wo_ref, lng_ref, lnb_ref, o_ref, *, alpha):
    d = h_ref.shape[-1]
    oa = oaf_ref[...] + oab_ref[...]
    gz = gz_ref[...].astype(F32)
    ya = []
    for hh in range(GDN_HEADS):
        x = oa[:, hh * GDN_DV:(hh + 1) * GDN_DV]
        ms = jnp.mean(x * x, axis=-1, keepdims=True)
        ya.append(x * lax.rsqrt(ms + RMS_EPS) * gng_ref[...] * gz[:, hh * GDN_DV:(hh + 1) * GDN_DV])
    ya = jnp.concatenate(ya, axis=-1).astype(BF16)
    ob = obf_ref[...] + obb_ref[...]
    rg = rg_ref[...].astype(F32)
    yb = []
    for hh in range(RET_HEADS):
        sl = slice(hh * RET_DV, (hh + 1) * RET_DV)
        x = ob[:, sl]
        mu = jnp.mean(x, axis=-1, keepdims=True)
        xc = x - mu
        var = jnp.mean(xc * xc, axis=-1, keepdims=True)
        yb.append(xc * lax.rsqrt(var + LN_EPS) * rng_ref[:, sl] * rg[:, sl])
    yb = jnp.concatenate(yb, axis=-1).astype(BF16)
    oc = (ocf_ref[...] + ocb_ref[...] + sd_ref[...] * sx_ref[...].astype(F32)) * sz_ref[...].astype(F32)
    gw = SSD_DINNER // SSD_GROUPS
    yc = []
    for gg in range(SSD_GROUPS):
        sl = slice(gg * gw, (gg + 1) * gw)
        x = oc[:, sl]
        ms = jnp.mean(x * x, axis=-1, keepdims=True)
        yc.append(x * lax.rsqrt(ms + RMS_EPS) * sng_ref[:, sl])
    yc = jnp.concatenate(yc, axis=-1).astype(BF16)
    gates = brg_ref[...].astype(F32)
    merged = (gates[:, 0:d] * _dot(ya, wa_ref[...])
              + gates[:, d:2 * d] * _dot(yb, wb_ref[...])
              + gates[:, 2 * d:3 * d] * _dot(yc, wc_ref[...]))
    mix = _dot(merged.astype(BF16), wo_ref[...])
    h = h_ref[...]
    y = alpha * h + mod_ref[MOD_MIX + 2:MOD_MIX + 3, :] * mix
    o_ref[...] = _layer_norm_rows(y, lng_ref[...], lnb_ref[...])


def mixer_merge(h, mod, o, proj, sx, lp, ln_g, ln_b, *, alpha, tm):
    b_, seq, d = h.shape
    tm = min(tm, seq)

    def tok(w):
        return pl.BlockSpec((None, tm, w), lambda b, i: (b, i, 0))

    ssd_d_cols = jnp.repeat(lp["ssd_d"].astype(F32), SSD_HEADDIM).reshape(1, SSD_DINNER)
    args = [h, mod, o["a_f"], o["a_b"], o["b_f"], o["b_b"], o["c_f"], o["c_b"], sx,
            proj["g_z"], proj["r_g"], proj["s_z"], proj["br_gate"],
            lp["gdn_norm_g"].astype(F32).reshape(1, GDN_DV),
            lp["ret_norm_g"].astype(F32).reshape(1, RET_HEADS * RET_DV),
            ssd_d_cols, lp["ssd_norm_g"].astype(F32).reshape(1, SSD_DINNER),
            lp["w_br_a"], lp["w_br_b"], lp["w_br_c"], lp["w_out"], ln_g.reshape(1, d), ln_b.reshape(1, d)]
    in_specs = [tok(d), pl.BlockSpec((None, N_ADA, d), lambda b, i: (b, 0, 0)),
                tok(GDN_HEADS * GDN_DV), tok(GDN_HEADS * GDN_DV),
                tok(RET_HEADS * RET_DV), tok(RET_HEADS * RET_DV),
                tok(SSD_DINNER), tok(SSD_DINNER), tok(SSD_DINNER),
                tok(GDN_HEADS * GDN_DV), tok(RET_HEADS * RET_DV), tok(SSD_DINNER), tok(N_BRANCH * d)]
    lead = (lp["layer"],)
    in_specs += [_resident_slice(a, lead) if a.ndim == 3 else _resident(a.shape) for a in args[13:]]
    return pl.pallas_call(
        functools.partial(_merge_body, alpha=alpha),
        grid=(b_, seq // tm),
        in_specs=in_specs,
        out_specs=tok(d),
        out_shape=jax.ShapeDtypeStruct(h.shape, F32),
        compiler_params=_cparams(2),
        name="mixer_merge",
    )(*args)


def kernel(x, c, ctx, c_ctx, ada_w, ada_b, ln_g, ln_b, ffn_w13, ffn_w2, mix_w_in,
           gdn_conv_w, gdn_a_log, gdn_dt_bias, gdn_norm_g, ret_decay, ret_norm_g,
           ssd_conv_w, ssd_conv_b, ssd_a_log, ssd_dt_bias, ssd_d, ssd_norm_g,
           w_br_a, w_br_b, w_br_c, mix_w_out):
    depth = ada_w.shape[0]
    alpha = float((2 * depth) ** 0.25)
    b_, seq, d = x.shape
    rope2 = rope_tables(seq)
    c_rows = jnp.concatenate([c, c_ctx[None, :]], axis=0)
    h, hc = x, ctx
    w13, w2 = ffn_w13.astype(BF16), ffn_w2.astype(BF16)
    w_in_r = split_w_in(mix_w_in, d)
    w_branch = {'w_br_a': w_br_a.astype(BF16), 'w_br_b': w_br_b.astype(BF16),
                'w_br_c': w_br_c.astype(BF16), 'w_out': mix_w_out.astype(BF16)}
    ada_b3 = ada_b.reshape(depth, 1, -1)
    for i in range(depth):
        last = i == depth - 1
        mod_all = ada_modulation(c_rows, ada_w, ada_b3, i)
        mod = mod_all[:b_].reshape(b_, N_ADA, d)
        mod_c = jnp.broadcast_to(mod_all[b_:].reshape(1, N_ADA, d), (b_, N_ADA, d))
        lp = {'layer': i, 'gdn_conv_w': gdn_conv_w[i], 'gdn_a_log': gdn_a_log[i],
              'gdn_dt_bias': gdn_dt_bias[i], 'gdn_norm_g': gdn_norm_g[i], 'ret_decay': ret_decay[i],
              'ret_norm_g': ret_norm_g[i], 'ssd_conv_w': ssd_conv_w[i], 'ssd_conv_b': ssd_conv_b[i],
              'ssd_a_log': ssd_a_log[i], 'ssd_dt_bias': ssd_dt_bias[i], 'ssd_d': ssd_d[i],
              'ssd_norm_g': ssd_norm_g[i], **w_branch}
        ffn = functools.partial(ffn_sublayer, alpha=alpha, tm=FFN_TILE)
        h = ffn(h, mod, w13, w2, (i, 0), ln_g[i, 0], ln_b[i, 0], mod_base=MOD_FFN1)
        hc = ffn(hc, mod_c, w13, w2, (i, 0), ln_g[i, 0], ln_b[i, 0], mod_base=MOD_FFN1)
        proj_c, prep_c = mixer_in_projection_prep(hc, mod_c, w_in_r, None, lp, tm=INPROJ_TILE)
        o_c, ctx_states = token_mixers(proj_c, prep_c, zero_states(b_), lp, not last)
        proj, prep = mixer_in_projection_prep(h, mod, w_in_r, rope2, lp, tm=INPROJ_TILE)
        o_l, _ = token_mixers(proj, prep, ctx_states, lp, True)
        h = mixer_merge(h, mod, o_l, proj, prep["sx"], lp, ln_g[i, 1], ln_b[i, 1], alpha=alpha, tm=MERGE_TILE)
        h = ffn(h, mod, w13, w2, (i, 1), ln_g[i, 2], ln_b[i, 2], mod_base=MOD_FFN2)
        if not last:
            hc = mixer_merge(hc, mod_c, o_c, proj_c, prep_c["sx"], lp, ln_g[i, 1], ln_b[i, 1],
                             alpha=alpha, tm=MERGE_TILE)
            hc = ffn(hc, mod_c, w13, w2, (i, 1), ln_g[i, 2], ln_b[i, 2], mod_base=MOD_FFN2)
    return h
```
